```python
import jax, jax.numpy as jnp
from jax import lax
import numpy as np

D_MODEL = 1024
BATCH = 8
SEQ = 2048
DEPTH = 2

CHUNK = 64
D_MIX = D_MODEL
D_A = D_MIX // 2
D_B = D_MIX - D_A
GMLP_BLOCK = 128
N_HEADS_A = 8
HEAD_DIM_A = D_A // N_HEADS_A
CONV_WIDTH = 31
D_IN = 2 * D_A + 2 * D_B
N_EXPERTS = 8
TOP_K = 2
D_FF = 3584
N_DENSE = (DEPTH + 1) // 2
N_MOE = DEPTH // 2
EPS = 1e-6

kernel_name = "hybrid_gmlp_conformer_moe_block"


def rms_norm(x, g):
    x32 = x.astype(jnp.float32)
    y = x32 * lax.rsqrt(jnp.mean(x32 * x32, axis=-1, keepdims=True) + EPS)
    return y.astype(x.dtype) * g


def layer_norm(x, g, b):
    x32 = x.astype(jnp.float32)
    mu = jnp.mean(x32, axis=-1, keepdims=True)
    xc = x32 - mu
    y = xc * lax.rsqrt(jnp.mean(xc * xc, axis=-1, keepdims=True) + EPS)
    return y.astype(x.dtype) * g + b


def spatial_mask():
    cid = jnp.arange(GMLP_BLOCK) // CHUNK
    return cid[:, None] >= cid[None, :]


def gmlp_mixer(u, v, ln_g, ln_b, w_s, b_s):
    bsz, seq, _ = u.shape
    u = jax.nn.gelu(u)
    v = layer_norm(jax.nn.gelu(v), ln_g, ln_b)
    vb = v.reshape(bsz, seq // GMLP_BLOCK, GMLP_BLOCK, N_HEADS_A, HEAD_DIM_A)
    ws = jnp.where(spatial_mask()[None], w_s, jnp.zeros_like(w_s))
    mixed = jnp.einsum('hts,bnshd->bnthd', ws, vb) + b_s.T[:, :, None]
    return u * mixed.reshape(bsz, seq, D_A)


def conv_mixer(a, g, conv_w, conv_b, ln_g, ln_b):
    xg = a * jax.nn.sigmoid(g)
    y = lax.conv_general_dilated(
        xg, conv_w[:, None, :], window_strides=(1,),
        padding=[(CONV_WIDTH - 1, 0)],
        dimension_numbers=('NWC', 'WIO', 'NWC'),
        feature_group_count=D_B) + conv_b
    return jax.nn.silu(layer_norm(y, ln_g, ln_b))


def swiglu(h, wg, wu, wd):
    return (jax.nn.silu(h @ wg) * (h @ wu)) @ wd


def moe_swiglu(h, router_w, router_b, wg, wu, wd):
    logits = (h @ router_w).astype(jnp.float32) + router_b.astype(jnp.float32)
    top_vals, top_idx = lax.top_k(logits, TOP_K)
    top_w = jax.nn.softmax(top_vals, axis=-1)
    gates = jnp.sum(jax.nn.one_hot(top_idx, N_EXPERTS, dtype=jnp.float32) * top_w[..., None],
                    axis=-2).astype(h.dtype)
    out = jnp.zeros_like(h)
    for e in range(N_EXPERTS):
        out = out + gates[..., e:e + 1] * swiglu(h, wg[e], wu[e], wd[e])
    return out


def setup_inputs(seed: int = 0) -> dict:
    key = jax.random.key(seed)
    ks = jax.random.split(key, 24)
    f32 = jnp.float32
    nrm = lambda k, shape, scale: jax.random.normal(k, shape, f32) * scale
    return {
        "x": nrm(ks[0], (BATCH, SEQ, D_MODEL), 1.0),
        "c": nrm(ks[1], (BATCH, D_MODEL), 1.0),
        "w_ada": nrm(ks[2], (DEPTH, D_MODEL, 6 * D_MODEL), 0.5 * D_MODEL ** -0.5),
        "b_ada": nrm(ks[3], (DEPTH, 6 * D_MODEL), 0.02),
        "norm_gain": 1.0 + nrm(ks[4], (DEPTH, 4, D_MODEL), 0.05),
        "w_in": nrm(ks[5], (DEPTH, D_MODEL, D_IN), D_MODEL ** -0.5),
        "b_in": nrm(ks[6], (DEPTH, D_IN), 0.02),
        "ln_v_gain": 1.0 + nrm(ks[7], (DEPTH, D_A), 0.05),
        "ln_v_bias": nrm(ks[8], (DEPTH, D_A), 0.02),
        "w_spatial": nrm(ks[9], (DEPTH, N_HEADS_A, GMLP_BLOCK, GMLP_BLOCK), GMLP_BLOCK ** -0.5),
        "b_spatial": 1.0 + nrm(ks[10], (DEPTH, N_HEADS_A, GMLP_BLOCK), 0.05),
        "conv_w": nrm(ks[11], (DEPTH, CONV_WIDTH, D_B), CONV_WIDTH ** -0.5),
        "conv_b": nrm(ks[12], (DEPTH, D_B), 0.02),
        "ln_conv_gain": 1.0 + nrm(ks[13], (DEPTH, D_B), 0.05),
        "ln_conv_bias": nrm(ks[14], (DEPTH, D_B), 0.02),
        "group_gain": 1.0 + nrm(ks[15], (DEPTH, D_MIX), 0.05),
        "w_out": nrm(ks[16], (DEPTH, D_MIX, D_MODEL), D_MIX ** -0.5),
        "ffn_w_gate": nrm(ks[17], (N_DENSE, D_MODEL, D_FF), D_MODEL ** -0.5),
        "ffn_w_up": nrm(ks[18], (N_DENSE, D_MODEL, D_FF), D_MODEL ** -0.5),
        "ffn_w_down": nrm(ks[19], (N_DENSE, D_FF, D_MODEL), D_FF ** -0.5),
        "router_w": nrm(ks[20], (N_MOE, D_MODEL, N_EXPERTS), D_MODEL ** -0.5),
        "router_b": nrm(ks[21], (N_MOE, N_EXPERTS), 0.01),
        "moe_w_gate": nrm(ks[22], (N_MOE, N_EXPERTS, D_MODEL, D_FF), D_MODEL ** -0.5),
        "moe_w_up": nrm(jax.random.fold_in(ks[23], 0), (N_MOE, N_EXPERTS, D_MODEL, D_FF), D_MODEL ** -0.5),
        "moe_w_down": nrm(jax.random.fold_in(ks[23], 1), (N_MOE, N_EXPERTS, D_FF, D_MODEL), D_FF ** -0.5),
    }


def reference(x, c, w_ada, b_ada, norm_gain, w_in, b_in, ln_v_gain, ln_v_bias, w_spatial,
              b_spatial, conv_w, conv_b, ln_conv_gain, ln_conv_bias, group_gain, w_out,
              ffn_w_gate, ffn_w_up, ffn_w_down, router_w, router_b, moe_w_gate, moe_w_up,
              moe_w_down):
    c_act = jax.nn.silu(c)
    for l in range(DEPTH):
        mod = (c_act @ w_ada[l] + b_ada[l])[:, None, :]
        sh1, sc1, g1, sh2, sc2, g2 = jnp.split(mod, 6, axis=-1)

        h = rms_norm(x, norm_gain[l, 0]) * (1.0 + sc1) + sh1
        z = h @ w_in[l] + b_in[l]
        ua, va, ab, gb = jnp.split(z, [D_A, 2 * D_A, 2 * D_A + D_B], axis=-1)
        ya = gmlp_mixer(ua, va, ln_v_gain[l], ln_v_bias[l], w_spatial[l], b_spatial[l])
        yb = conv_mixer(ab, gb, conv_w[l], conv_b[l], ln_conv_gain[l], ln_conv_bias[l])
        y = jnp.concatenate([rms_norm(ya, group_gain[l, :D_A]),
                             rms_norm(yb, group_gain[l, D_A:])], axis=-1) @ w_out[l]
        x = x + g1 * rms_norm(y, norm_gain[l, 1])

        h = rms_norm(x, norm_gain[l, 2]) * (1.0 + sc2) + sh2
        if l % 2 == 0:
            i = l // 2
            f = swiglu(h, ffn_w_gate[i], ffn_w_up[i], ffn_w_down[i])
        else:
            i = l // 2
            f = moe_swiglu(h, router_w[i], router_b[i], moe_w_gate[i], moe_w_up[i], moe_w_down[i])
        x = x + g2 * rms_norm(f, norm_gain[l, 3])
    return x
```

```python
import functools

import jax
import jax.numpy as jnp
from jax import lax
from jax.experimental import pallas as pl
from jax.experimental.pallas import tpu as pltpu

F32 = jnp.float32
BF16 = jnp.bfloat16
I32 = jnp.int32

EPS = 1e-6
CHUNK = 64
GMLP_BLOCK = 128
N_HEADS_A = 8
CONV_WIDTH = 31
N_EXPERTS = 8
TOP_K = 2

LANES = 128
SUBLANES = 8
CONV_HALO = 32
VMEM_LIMIT = 56 * 1024 * 1024

SEQ_TILE = 256
FFN_ROWS = 512
FFN_COLS = 512
ROUTE_ROWS = 512
MOE_ROWS = 512
MOVE_ROWS = 256


def _rms(x, g):
    return x * lax.rsqrt(jnp.mean(x * x, axis=-1, keepdims=True) + EPS) * g


def _layer_norm(x, g, b):
    mu = jnp.mean(x, axis=-1, keepdims=True)
    xc = x - mu
    return xc * lax.rsqrt(jnp.mean(xc * xc, axis=-1, keepdims=True) + EPS) * g + b


def _cparams(sem, vmem=VMEM_LIMIT):
    return pltpu.CompilerParams(dimension_semantics=sem, vmem_limit_bytes=vmem)


def _ada_kernel(c_ref, w_ref, b_ref, o_ref):
    c_act = jax.nn.silu(c_ref[...])
    o_ref[0] = jnp.dot(c_act, w_ref[0], preferred_element_type=F32,
                       precision=lax.Precision.HIGHEST) + b_ref[0]


def _ada(c, w_ada, b_ada):
    depth, d, n6 = w_ada.shape
    bsz = c.shape[0]
    nc = 1536
    return pl.pallas_call(
        _ada_kernel,
        grid=(depth, n6 // nc),
        in_specs=[pl.BlockSpec((bsz, d), lambda l, j: (0, 0)),
                  pl.BlockSpec((1, d, nc), lambda l, j: (l, 0, j)),
                  pl.BlockSpec((1, 1, nc), lambda l, j: (l, 0, j))],
        out_specs=pl.BlockSpec((1, bsz, nc), lambda l, j: (l, 0, j)),
        out_shape=jax.ShapeDtypeStruct((depth, bsz, n6), F32),
        compiler_params=_cparams(("arbitrary", "arbitrary")),
        name="ada_mod",
    )(c, w_ada, b_ada.reshape(depth, 1, n6))


def _mix_kernel(x_ref, mod_ref, ng_ref, w_in_ref, b_in_ref, lnv_g_ref, lnv_b_ref, ws_ref,
                bs_ref, cw_ref, cb_ref, lnc_g_ref, lnc_b_ref, gg_ref, w_out_ref,
                o_ref, wsp_scr, xg_scr, yc_scr):
    ts = x_ref.shape[1]
    d_a = lnv_g_ref.shape[1]
    d_b = lnc_g_ref.shape[1]
    b = pl.program_id(0)
    s = pl.program_id(1)

    @pl.when(jnp.logical_and(b == 0, s == 0))
    def _():
        t_chunk = lax.broadcasted_iota(I32, (GMLP_BLOCK, GMLP_BLOCK), 0) // CHUNK
        s_chunk = lax.broadcasted_iota(I32, (GMLP_BLOCK, GMLP_BLOCK), 1) // CHUNK
        allowed = t_chunk >= s_chunk
        for j in range(N_HEADS_A // 2):
            lo = jnp.where(allowed, ws_ref[2 * j], 0.0).astype(BF16)
            hi = jnp.where(allowed, ws_ref[2 * j + 1], 0.0).astype(BF16)
            wsp_scr[j] = jnp.concatenate([lo, hi], axis=1)

    x = x_ref[0]
    sh1 = mod_ref[0, 0:1, :]
    sc1 = mod_ref[0, 1:2, :]
    g1 = mod_ref[0, 2:3, :]
    h = _rms(x, ng_ref[0:1, :]) * (1.0 + sc1) + sh1
    z = jnp.dot(h.astype(BF16), w_in_ref[...], preferred_element_type=F32) + b_in_ref[...]
    ua = z[:, 0:d_a]
    va = z[:, d_a:2 * d_a]
    ab = z[:, 2 * d_a:2 * d_a + d_b]
    gb = z[:, 2 * d_a + d_b:]

    u = jax.nn.gelu(ua)
    v = _layer_norm(jax.nn.gelu(va), lnv_g_ref[...], lnv_b_ref[...]).astype(BF16)
    head_dim = d_a // N_HEADS_A
    lane = lax.broadcasted_iota(I32, (GMLP_BLOCK, LANES), 1)
    first_head = lane < head_dim
    zero = jnp.zeros((GMLP_BLOCK, LANES), BF16)
    blocks = []
    for n in range(ts // GMLP_BLOCK):
        cols = []
        for j in range(d_a // LANES):
            vc = v[n * GMLP_BLOCK:(n + 1) * GMLP_BLOCK, j * LANES:(j + 1) * LANES]
            rhs = jnp.concatenate([jnp.where(first_head, vc, zero),
                                   jnp.where(first_head, zero, vc)], axis=0)
            cols.append(jnp.dot(wsp_scr[j], rhs, preferred_element_type=F32))
        blocks.append(jnp.concatenate(cols, axis=1) + bs_ref[...])
    mixed = jnp.concatenate(blocks, axis=0)
    ya = u * mixed

    xg = ab * jax.nn.sigmoid(gb)

    @pl.when(s == 0)
    def _():
        xg_scr[0:CONV_HALO, :] = jnp.zeros((CONV_HALO, d_b), F32)

    xg_scr[CONV_HALO:CONV_HALO + ts, :] = xg
    first_tap = CONV_HALO - (CONV_WIDTH - 1)
    rows = 64
    for rc in range(ts // rows):
        for lc in range(d_b // LANES):
            ls = slice(lc * LANES, (lc + 1) * LANES)
            acc = jnp.broadcast_to(cb_ref[:, ls], (rows, LANES))
            for k in range(CONV_WIDTH):
                r0 = rc * rows + first_tap + k
                acc = acc + cw_ref[k:k + 1, ls] * xg_scr[r0:r0 + rows, ls]
            yc_scr[rc * rows:(rc + 1) * rows, ls] = acc
    xg_scr[0:CONV_HALO, :] = xg_scr[ts:ts + CONV_HALO, :]
    yb = jax.nn.silu(_layer_norm(yc_scr[...], lnc_g_ref[...], lnc_b_ref[...]))

    ycat = jnp.concatenate([_rms(ya, gg_ref[:, 0:d_a]), _rms(yb, gg_ref[:, d_a:])], axis=1)
    y = jnp.dot(ycat.astype(BF16), w_out_ref[...], preferred_element_type=F32)
    o_ref[0] = x + g1 * _rms(y, ng_ref[1:2, :])


def _mix(x, mod, ng, w_in, b_in, lnv_g, lnv_b, w_sp, b_sp, cw, cb, lnc_g, lnc_b, gg, w_out):
    bsz, seq, d = x.shape
    d_in = w_in.shape[1]
    d_a = lnv_g.shape[0]
    d_b = lnc_g.shape[0]
    ts = min(SEQ_TILE, seq)
    assert seq % ts == 0 and ts % GMLP_BLOCK == 0 and ts >= CONV_HALO
    bs_full = jnp.repeat(b_sp.T, d_a // N_HEADS_A, axis=1)
    const = lambda *shape: pl.BlockSpec(shape, lambda b, s: (0,) * len(shape))
    return pl.pallas_call(
        _mix_kernel,
        grid=(bsz, seq // ts),
        in_specs=[pl.BlockSpec((1, ts, d), lambda b, s: (b, s, 0)),
                  pl.BlockSpec((1, 6, d), lambda b, s: (b, 0, 0)),
                  const(4, d), const(d, d_in), const(1, d_in), const(1, d_a), const(1, d_a),
                  const(N_HEADS_A, GMLP_BLOCK, GMLP_BLOCK), const(GMLP_BLOCK, d_a),
                  const(CONV_WIDTH, d_b), const(1, d_b), const(1, d_b), const(1, d_b),
                  const(1, d_a + d_b), const(d_a + d_b, d)],
        out_specs=pl.BlockSpec((1, ts, d), lambda b, s: (b, s, 0)),
        out_shape=jax.ShapeDtypeStruct((bsz, seq, d), F32),
        scratch_shapes=[pltpu.VMEM((N_HEADS_A // 2, GMLP_BLOCK, 2 * GMLP_BLOCK), BF16),
                        pltpu.VMEM((ts + CONV_HALO, d_b), F32),
                        pltpu.VMEM((ts, d_b), F32)],
        compiler_params=_cparams(("arbitrary", "arbitrary")),
        name="token_mix",
    )(x, mod, ng, w_in.astype(BF16), b_in.reshape(1, d_in), lnv_g.reshape(1, d_a),
      lnv_b.reshape(1, d_a), w_sp, bs_full, cw, cb.reshape(1, d_b), lnc_g.reshape(1, d_b),
      lnc_b.reshape(1, d_b), gg.reshape(1, d_a + d_b), w_out.astype(BF16))


def _swiglu_chunk(h, wg, wu, wd):
    g = jnp.dot(h, wg, preferred_element_type=F32)
    u = jnp.dot(h, wu, preferred_element_type=F32)
    a = (jax.nn.silu(g) * u).astype(BF16)
    return jnp.dot(a, wd, preferred_element_type=F32)


def _ffn_kernel(x_ref, mod_ref, ng_ref, wg_ref, wu_ref, wd_ref, o_ref, h_scr, acc_scr):
    j = pl.program_id(1)

    @pl.when(j == 0)
    def _():
        sh2 = mod_ref[0, 3:4, :]
        sc2 = mod_ref[0, 4:5, :]
        h = _rms(x_ref[...], ng_ref[2:3, :]) * (1.0 + sc2) + sh2
        h_scr[...] = h.astype(BF16)
        acc_scr[...] = jnp.zeros_like(acc_scr)

    acc_scr[...] += _swiglu_chunk(h_scr[...], wg_ref[...], wu_ref[...], wd_ref[...])

    @pl.when(j == pl.num_programs(1) - 1)
    def _():
        g2 = mod_ref[0, 5:6, :]
        o_ref[...] = x_ref[...] + g2 * _rms(acc_scr[...], ng_ref[3:4, :])


def _ffn(x2d, seq, mod, ng, wg, wu, wd):
    n, d = x2d.shape
    f = wg.shape[1]
    tm = min(FFN_ROWS, seq)
    fc = FFN_COLS
    assert seq % tm == 0 and f % fc == 0
    per_seq = seq // tm
    return pl.pallas_call(
        _ffn_kernel,
        grid=(n // tm, f // fc),
        in_specs=[pl.BlockSpec((tm, d), lambda i, j: (i, 0)),
                  pl.BlockSpec((1, 6, d), lambda i, j: (i // per_seq, 0, 0)),
                  pl.BlockSpec((4, d), lambda i, j: (0, 0)),
                  pl.BlockSpec((d, fc), lambda i, j: (0, j)),
                  pl.BlockSpec((d, fc), lambda i, j: (0, j)),
                  pl.BlockSpec((fc, d), lambda i, j: (j, 0))],
        out_specs=pl.BlockSpec((tm, d), lambda i, j: (i, 0)),
        out_shape=jax.ShapeDtypeStruct((n, d), F32),
        scratch_shapes=[pltpu.VMEM((tm, d), BF16), pltpu.VMEM((tm, d), F32)],
        compiler_params=_cparams(("arbitrary", "arbitrary")),
        name="ffn_dense",
    )(x2d, mod, ng, wg.astype(BF16), wu.astype(BF16), wd.astype(BF16))


def _route_kernel(x_ref, mod_ref, ng_ref, rw_ref, rb_ref, h_ref, eid_ref, gate_ref, rank_ref,
                  cnt_ref, run_scr):
    i = pl.program_id(0)
    tr = x_ref.shape[0]

    @pl.when(i == 0)
    def _():
        run_scr[...] = jnp.zeros_like(run_scr)

    sh2 = mod_ref[0, 3:4, :]
    sc2 = mod_ref[0, 4:5, :]
    h = _rms(x_ref[...], ng_ref[2:3, :]) * (1.0 + sc2) + sh2
    h_ref[...] = h
    logits = lax.dot_general(rw_ref[...], h, (((1,), (1,)), ((), ())),
                             preferred_element_type=F32,
                             precision=lax.Precision.HIGHEST) + rb_ref[...]
    e_iota = lax.broadcasted_iota(I32, logits.shape, 0)
    m1 = jnp.max(logits, axis=0, keepdims=True)
    i1 = jnp.min(jnp.where(logits == m1, e_iota, N_EXPERTS), axis=0, keepdims=True)
    oh1 = e_iota == i1
    rest = jnp.where(oh1, -jnp.inf, logits)
    m2 = jnp.max(rest, axis=0, keepdims=True)
    i2 = jnp.min(jnp.where(rest == m2, e_iota, N_EXPERTS), axis=0, keepdims=True)
    oh2 = e_iota == i2
    e2 = jnp.exp(m2 - m1)
    den = 1.0 + e2
    gate_ref[...] = jnp.concatenate([1.0 / den, e2 / den], axis=0)
    eid_ref[...] = jnp.concatenate([i1, i2], axis=0)

    member = oh1.astype(F32) + oh2.astype(F32)
    before = (lax.broadcasted_iota(I32, (tr, tr), 0) <
              lax.broadcasted_iota(I32, (tr, tr), 1)).astype(BF16)
    prefix = jnp.dot(member.astype(BF16), before, preferred_element_type=F32) + run_scr[:, 0:1]
    r1 = jnp.sum(jnp.where(oh1, prefix, 0.0), axis=0, keepdims=True)
    r2 = jnp.sum(jnp.where(oh2, prefix, 0.0), axis=0, keepdims=True)
    rank_ref[...] = jnp.concatenate([r1, r2], axis=0).astype(I32)
    run_scr[...] += jnp.sum(member, axis=1, keepdims=True)
    cnt_ref[...] = run_scr[...].astype(I32)


def _route(x2d, seq, mod, ng, router_w, router_b):
    n, d = x2d.shape
    tr = min(ROUTE_ROWS, seq)
    assert seq % tr == 0
    per_seq = seq // tr
    return pl.pallas_call(
        _route_kernel,
        grid=(n // tr,),
        in_specs=[pl.BlockSpec((tr, d), lambda i: (i, 0)),
                  pl.BlockSpec((1, 6, d), lambda i: (i // per_seq, 0, 0)),
                  pl.BlockSpec((4, d), lambda i: (0, 0)),
                  pl.BlockSpec((N_EXPERTS, d), lambda i: (0, 0)),
                  pl.BlockSpec((N_EXPERTS, 1), lambda i: (0, 0))],
        out_specs=[pl.BlockSpec((tr, d), lambda i: (i, 0)),
                   pl.BlockSpec((TOP_K, tr), lambda i: (0, i)),
                   pl.BlockSpec((TOP_K, tr), lambda i: (0, i)),
                   pl.BlockSpec((TOP_K, tr), lambda i: (0, i)),
                   pl.BlockSpec((N_EXPERTS, LANES), lambda i: (0, 0))],
        out_shape=[jax.ShapeDtypeStruct((n, d), F32),
                   jax.ShapeDtypeStruct((TOP_K, n), I32),
                   jax.ShapeDtypeStruct((TOP_K, n), F32),
                   jax.ShapeDtypeStruct((TOP_K, n), I32),
                   jax.ShapeDtypeStruct((N_EXPERTS, LANES), I32)],
        scratch_shapes=[pltpu.VMEM((N_EXPERTS, LANES), F32)],
        compiler_params=_cparams(("arbitrary",)),
        name="moe_route",
    )(x2d, mod, ng, router_w.T, router_b.reshape(N_EXPERTS, 1))


def _row_copy(src_ref, src_row, dst_ref, dst_row, sem):
    return pltpu.make_async_copy(src_ref.at[pl.ds(src_row, 1), :],
                                 dst_ref.at[pl.ds(dst_row, 1), :], sem)


def _dispatch_kernel(pos_ref, h_ref, init_ref, hs_ref, sem):
    del init_ref
    rows = h_ref.shape[0]

    def start(r, c):
        for k in range(TOP_K):
            _row_copy(h_ref, r, hs_ref, pos_ref[k, r], sem).start()
        return c

    lax.fori_loop(0, rows, start, 0)

    def wait(r, c):
        for k in range(TOP_K):
            _row_copy(h_ref, r, hs_ref, pos_ref[k, r], sem).wait()
        return c

    lax.fori_loop(0, rows, wait, 0)


def _dispatch(pos, h2d, n_rows):
    n, d = h2d.shape
    tm = min(MOVE_ROWS, n)
    assert n % tm == 0
    return pl.pallas_call(
        _dispatch_kernel,
        grid=(n // tm,),
        in_specs=[pl.BlockSpec((TOP_K, tm), lambda i: (0, i), memory_space=pltpu.SMEM),
                  pl.BlockSpec((tm, d), lambda i: (i, 0)),
                  pl.BlockSpec(memory_space=pl.ANY)],
        out_specs=pl.BlockSpec(memory_space=pl.ANY),
        out_shape=jax.ShapeDtypeStruct((n_rows, d), F32),
        scratch_shapes=[pltpu.SemaphoreType.DMA],
        input_output_aliases={2: 0},
        compiler_params=_cparams(("arbitrary",)),
        name="moe_dispatch",
    )(pos, h2d, jnp.zeros((n_rows, d), F32))


def _expert_kernel(te_ref, nv_ref, hs_ref, wg_ref, wu_ref, wd_ref, y_ref, h_scr, acc_scr):
    del te_ref
    i = pl.program_id(0)
    j = pl.program_id(1)

    @pl.when(i < nv_ref[0])
    def _():
        @pl.when(j == 0)
        def _():
            h_scr[...] = hs_ref[...].astype(BF16)
            acc_scr[...] = jnp.zeros_like(acc_scr)

        acc_scr[...] += _swiglu_chunk(h_scr[...], wg_ref[0], wu_ref[0], wd_ref[0])

        @pl.when(j == pl.num_programs(1) - 1)
        def _():
            y_ref[...] = acc_scr[...]

    @pl.when(jnp.logical_and(i >= nv_ref[0], j == pl.num_programs(1) - 1))
    def _():
        y_ref[...] = jnp.zeros_like(y_ref)


def _experts(tile_expert, n_valid, hs, wg, wu, wd):
    n_rows, d = hs.shape
    f = wg.shape[2]
    tm = MOE_ROWS
    fc = FFN_COLS
    n_tiles = n_rows // tm
    n_fc = f // fc

    def row_map(i, j, te, nv):
        return (jnp.minimum(i, nv[0] - 1), 0)

    def col(i, j, nv):
        return jnp.where(i < nv[0], j, n_fc - 1)

    def tile_e(i, te, nv):
        return te[jnp.minimum(i, nv[0] - 1)]

    grid_spec = pltpu.PrefetchScalarGridSpec(
        num_scalar_prefetch=2,
        grid=(n_tiles, n_fc),
        in_specs=[pl.BlockSpec((tm, d), row_map),
                  pl.BlockSpec((1, d, fc), lambda i, j, te, nv: (tile_e(i, te, nv), 0, col(i, j, nv))),
                  pl.BlockSpec((1, d, fc), lambda i, j, te, nv: (tile_e(i, te, nv), 0, col(i, j, nv))),
                  pl.BlockSpec((1, fc, d), lambda i, j, te, nv: (tile_e(i, te, nv), col(i, j, nv), 0))],
        out_specs=pl.BlockSpec((tm, d), lambda i, j, te, nv: (i, 0)),
        scratch_shapes=[pltpu.VMEM((tm, d), BF16), pltpu.VMEM((tm, d), F32)],
    )
    return pl.pallas_call(
        _expert_kernel,
        grid_spec=grid_spec,
        out_shape=jax.ShapeDtypeStruct((n_rows, d), F32),
        compiler_params=_cparams(("arbitrary", "arbitrary")),
        name="moe_experts",
    )(tile_expert, n_valid, hs, wg.astype(BF16), wu.astype(BF16), wd.astype(BF16))


def _combine_kernel(pos_ref, gate_ref, x_ref, mod_ref, ng_ref, y_ref, o_ref, buf, sem):
    rows = x_ref.shape[0]

    def start(r, c):
        for k in range(TOP_K):
            _row_copy(y_ref, pos_ref[k, r], buf.at[k], r, sem).start()
        return c

    lax.fori_loop(0, rows, start, 0)

    def wait(r, c):
        for k in range(TOP_K):
            _row_copy(y_ref, pos_ref[k, r], buf.at[k], r, sem).wait()
        return c

    lax.fori_loop(0, rows, wait, 0)

    f = gate_ref[:, 0:1] * buf[0] + gate_ref[:, 1:2] * buf[1]
    g2 = mod_ref[0, 5:6, :]
    o_ref[...] = x_ref[...] + g2 * _rms(f, ng_ref[3:4, :])


def _combine(pos, gates, x2d, seq, mod, ng, y):
    n, d = x2d.shape
    tm = min(MOVE_ROWS, seq)
    assert seq % tm == 0
    per_seq = seq // tm
    return pl.pallas_call(
        _combine_kernel,
        grid=(n // tm,),
        in_specs=[pl.BlockSpec((TOP_K, tm), lambda i: (0, i), memory_space=pltpu.SMEM),
                  pl.BlockSpec((tm, TOP_K), lambda i: (i, 0)),
                  pl.BlockSpec((tm, d), lambda i: (i, 0)),
                  pl.BlockSpec((1, 6, d), lambda i: (i // per_seq, 0, 0)),
                  pl.BlockSpec((4, d), lambda i: (0, 0)),
                  pl.BlockSpec(memory_space=pl.ANY)],
        out_specs=pl.BlockSpec((tm, d), lambda i: (i, 0)),
        out_shape=jax.ShapeDtypeStruct((n, d), F32),
        scratch_shapes=[pltpu.VMEM((TOP_K, tm, d), F32), pltpu.SemaphoreType.DMA],
        compiler_params=_cparams(("arbitrary",)),
        name="moe_combine",
    )(pos, gates, x2d, mod, ng, y)


def _moe(x2d, seq, mod, ng, router_w, router_b, wg, wu, wd):
    n, d = x2d.shape
    tm = MOE_ROWS
    h, eid, gate, rank, cnt = _route(x2d, seq, mod, ng, router_w, router_b)
    counts = cnt[:, 0]
    padded = ((counts + tm - 1) // tm) * tm
    ends = jnp.cumsum(padded)
    starts = ends - padded
    pos = starts[eid] + rank
    n_tiles = (TOP_K * n) // tm + N_EXPERTS
    tile_start = jnp.arange(n_tiles, dtype=I32) * tm
    tile_expert = jnp.minimum(jnp.sum(tile_start[:, None] >= ends[None, :], axis=1),
                              N_EXPERTS - 1).astype(I32)
    n_valid = (ends[-1:] // tm).astype(I32)
    hs = _dispatch(pos, h, n_tiles * tm)
    y = _experts(tile_expert, n_valid, hs, wg, wu, wd)
    return _combine(pos, gate.T, x2d, seq, mod, ng, y)


def kernel(x, c, w_ada, b_ada, norm_gain, w_in, b_in, ln_v_gain, ln_v_bias, w_spatial, b_spatial, conv_w, conv_b, ln_conv_gain, ln_conv_bias, group_gain, w_out, ffn_w_gate, ffn_w_up, ffn_w_down, router_w, router_b, moe_w_gate, moe_w_up, moe_w_down):
    bsz, seq, d = x.shape
    depth = w_ada.shape[0]
    mod_all = _ada(c, w_ada, b_ada).reshape(depth, bsz, 6, d)
    for l in range(depth):
        mod = mod_all[l]
        ng = norm_gain[l]
        x = _mix(x, mod, ng, w_in[l], b_in[l], ln_v_gain[l], ln_v_bias[l], w_spatial[l],
                 b_spatial[l], conv_w[l], conv_b[l], ln_conv_gain[l], ln_conv_bias[l],
                 group_gain[l], w_out[l])
        x2d = x.reshape(bsz * seq, d)
        i = l // 2
        if l % 2 == 0:
            x2d = _ffn(x2d, seq, mod, ng, ffn_w_gate[i], ffn_w_up[i], ffn_w_down[i])
        else:
            x2d = _moe(x2d, seq, mod, ng, router_w[i], router_b[i], moe_w_gate[i],
                       moe_w_up[i], moe_w_down[i])
        x = x2d.reshape(bsz, seq, d)
    return x
```

```python
import jax
import jax.numpy as jnp
from jax import lax
from jax.experimental import pallas as pl
from jax.experimental.pallas import tpu as pltpu

F32 = jnp.float32
BF16 = jnp.bfloat16
I32 = jnp.int32

EPS = 1e-6
CHUNK = 64
GMLP_BLOCK = 128
N_HEADS_A = 8
CONV_WIDTH = 31
N_EXPERTS = 8
TOP_K = 2

LANES = 128
SUBLANES = 8
CONV_HALO = 32
VMEM_LIMIT = 56 * 1024 * 1024

SEQ_TILE = 256
FFN_ROWS = 512
FFN_COLS = 896
MOE_ROWS = 512
MOVE_ROWS = 256
WINDOW = MOVE_ROWS // 2


def _rms(x, g):
    return x * lax.rsqrt(jnp.mean(x * x, axis=-1, keepdims=True) + EPS) * g


def _layer_norm(x, g, b):
    mu = jnp.mean(x, axis=-1, keepdims=True)
    xc = x - mu
    return xc * lax.rsqrt(jnp.mean(xc * xc, axis=-1, keepdims=True) + EPS) * g + b


def _cparams(sem, vmem=VMEM_LIMIT):
    return pltpu.CompilerParams(dimension_semantics=sem, vmem_limit_bytes=vmem)


def _ada_kernel(c_ref, w_ref, b_ref, o_ref):
    c_act = jax.nn.silu(c_ref[...])
    o_ref[0] = jnp.dot(c_act, w_ref[0], preferred_element_type=F32,
                       precision=lax.Precision.HIGHEST) + b_ref[0]


def _ada(c, w_ada, b_ada):
    depth, d, n6 = w_ada.shape
    bsz = c.shape[0]
    nc = 1536
    return pl.pallas_call(
        _ada_kernel,
        grid=(depth, n6 // nc),
        in_specs=[pl.BlockSpec((bsz, d), lambda l, j: (0, 0)),
                  pl.BlockSpec((1, d, nc), lambda l, j: (l, 0, j)),
                  pl.BlockSpec((1, 1, nc), lambda l, j: (l, 0, j))],
        out_specs=pl.BlockSpec((1, bsz, nc), lambda l, j: (l, 0, j)),
        out_shape=jax.ShapeDtypeStruct((depth, bsz, n6), F32),
        compiler_params=_cparams(("arbitrary", "arbitrary")),
        name="ada_mod",
    )(c, w_ada, b_ada.reshape(depth, 1, n6))


def _mix_kernel(x_ref, mod_ref, ng_ref, w_in_ref, b_in_ref, lnv_g_ref, lnv_b_ref, ws_ref,
                bs_ref, cw_ref, cb_ref, lnc_g_ref, lnc_b_ref, gg_ref, w_out_ref,
                o_ref, wsp_scr, xg_scr, sh_scr, yc_scr):
    ts = x_ref.shape[1]
    d_a = lnv_g_ref.shape[1]
    d_b = lnc_g_ref.shape[1]
    b = pl.program_id(0)
    s = pl.program_id(1)

    @pl.when(jnp.logical_and(b == 0, s == 0))
    def _():
        t_chunk = lax.broadcasted_iota(I32, (GMLP_BLOCK, GMLP_BLOCK), 0) // CHUNK
        s_chunk = lax.broadcasted_iota(I32, (GMLP_BLOCK, GMLP_BLOCK), 1) // CHUNK
        allowed = t_chunk >= s_chunk
        for j in range(N_HEADS_A // 2):
            lo = jnp.where(allowed, ws_ref[2 * j], 0.0).astype(BF16)
            hi = jnp.where(allowed, ws_ref[2 * j + 1], 0.0).astype(BF16)
            wsp_scr[j] = jnp.concatenate([lo, hi], axis=1)

    x = x_ref[0]
    sh1 = mod_ref[0, 0:1, :]
    sc1 = mod_ref[0, 1:2, :]
    g1 = mod_ref[0, 2:3, :]
    h = _rms(x, ng_ref[0:1, :]) * (1.0 + sc1) + sh1
    z = jnp.dot(h.astype(BF16), w_in_ref[...], preferred_element_type=F32) + b_in_ref[...]
    ua = z[:, 0:d_a]
    va = z[:, d_a:2 * d_a]
    ab = z[:, 2 * d_a:2 * d_a + d_b]
    gb = z[:, 2 * d_a + d_b:]

    u = jax.nn.gelu(ua)
    v = _layer_norm(jax.nn.gelu(va), lnv_g_ref[...], lnv_b_ref[...]).astype(BF16)
    head_dim = d_a // N_HEADS_A
    lane = lax.broadcasted_iota(I32, (GMLP_BLOCK, LANES), 1)
    first_head = lane < head_dim
    zero = jnp.zeros((GMLP_BLOCK, LANES), BF16)
    blocks = []
    for n in range(ts // GMLP_BLOCK):
        cols = []
        for j in range(d_a // LANES):
            vc = v[n * GMLP_BLOCK:(n + 1) * GMLP_BLOCK, j * LANES:(j + 1) * LANES]
            rhs = jnp.concatenate([jnp.where(first_head, vc, zero),
                                   jnp.where(first_head, zero, vc)], axis=0)
            cols.append(jnp.dot(wsp_scr[j], rhs, preferred_element_type=F32))
        blocks.append(jnp.concatenate(cols, axis=1) + bs_ref[...])
    mixed = jnp.concatenate(blocks, axis=0)
    ya = u * mixed

    xg = ab * jax.nn.sigmoid(gb)

    @pl.when(s == 0)
    def _():
        xg_scr[0:CONV_HALO, :] = jnp.zeros((CONV_HALO, d_b), F32)

    xg_scr[CONV_HALO:CONV_HALO + ts, :] = xg
    span = ts + CONV_HALO - SUBLANES
    for r in range(1, SUBLANES):
        sh_scr[r, 0:span, :] = xg_scr[r:r + span, :]
    first_tap = CONV_HALO - (CONV_WIDTH - 1)
    rows = 64
    for rc in range(ts // rows):
        for lc in range(d_b // LANES):
            ls = slice(lc * LANES, (lc + 1) * LANES)
            acc = jnp.broadcast_to(cb_ref[:, ls], (rows, LANES))
            for k in range(CONV_WIDTH):
                off = first_tap + k
                r = off % SUBLANES
                r0 = rc * rows + off - r
                if r == 0:
                    win = xg_scr[r0:r0 + rows, ls]
                else:
                    win = sh_scr[r, r0:r0 + rows, ls]
                acc = acc + cw_ref[k:k + 1, ls] * win
            yc_scr[rc * rows:(rc + 1) * rows, ls] = acc
    xg_scr[0:CONV_HALO, :] = xg_scr[ts:ts + CONV_HALO, :]
    yb = jax.nn.silu(_layer_norm(yc_scr[...], lnc_g_ref[...], lnc_b_ref[...]))

    ycat = jnp.concatenate([_rms(ya, gg_ref[:, 0:d_a]), _rms(yb, gg_ref[:, d_a:])], axis=1)
    y = jnp.dot(ycat.astype(BF16), w_out_ref[...], preferred_element_type=F32)
    o_ref[0] = x + g1 * _rms(y, ng_ref[1:2, :])


def _mix(x, mod, ng, w_in, b_in, lnv_g, lnv_b, w_sp, b_sp, cw, cb, lnc_g, lnc_b, gg, w_out):
    bsz, seq, d = x.shape
    d_in = w_in.shape[1]
    d_a = lnv_g.shape[0]
    d_b = lnc_g.shape[0]
    ts = min(SEQ_TILE, seq)
    assert seq % ts == 0 and ts % GMLP_BLOCK == 0 and ts >= CONV_HALO
    bs_full = jnp.repeat(b_sp.T, d_a // N_HEADS_A, axis=1)
    const = lambda *shape: pl.BlockSpec(shape, lambda b, s: (0,) * len(shape))
    return pl.pallas_call(
        _mix_kernel,
        grid=(bsz, seq // ts),
        in_specs=[pl.BlockSpec((1, ts, d), lambda b, s: (b, s, 0)),
                  pl.BlockSpec((1, 6, d), lambda b, s: (b, 0, 0)),
                  const(4, d), const(d, d_in), const(1, d_in), const(1, d_a), const(1, d_a),
                  const(N_HEADS_A, GMLP_BLOCK, GMLP_BLOCK), const(GMLP_BLOCK, d_a),
                  const(CONV_WIDTH, d_b), const(1, d_b), const(1, d_b), const(1, d_b),
                  const(1, d_a + d_b), const(d_a + d_b, d)],
        out_specs=pl.BlockSpec((1, ts, d), lambda b, s: (b, s, 0)),
        out_shape=jax.ShapeDtypeStruct((bsz, seq, d), F32),
        scratch_shapes=[pltpu.VMEM((N_HEADS_A // 2, GMLP_BLOCK, 2 * GMLP_BLOCK), BF16),
                        pltpu.VMEM((ts + CONV_HALO, d_b), F32),
                        pltpu.VMEM((SUBLANES, ts + CONV_HALO, d_b), F32),
                        pltpu.VMEM((ts, d_b), F32)],
        compiler_params=_cparams(("arbitrary", "arbitrary")),
        name="token_mix",
    )(x, mod, ng, w_in.astype(BF16), b_in.reshape(1, d_in), lnv_g.reshape(1, d_a),
      lnv_b.reshape(1, d_a), w_sp, bs_full, cw, cb.reshape(1, d_b), lnc_g.reshape(1, d_b),
      lnc_b.reshape(1, d_b), gg.reshape(1, d_a + d_b), w_out.astype(BF16))


def _swiglu_chunk(h, wg, wu, wd):
    g = jnp.dot(h, wg, preferred_element_type=F32)
    u = jnp.dot(h, wu, preferred_element_type=F32)
    a = (jax.nn.silu(g) * u).astype(BF16)
    return jnp.dot(a, wd, preferred_element_type=F32)


def _ffn_kernel(x_ref, mod_ref, ng_ref, wg_ref, wu_ref, wd_ref, o_ref, h_scr, acc_scr):
    j = pl.program_id(1)

    @pl.when(j == 0)
    def _():
        sh2 = mod_ref[0, 3:4, :]
        sc2 = mod_ref[0, 4:5, :]
        h = _rms(x_ref[...], ng_ref[2:3, :]) * (1.0 + sc2) + sh2
        h_scr[...] = h.astype(BF16)
        acc_scr[...] = jnp.zeros_like(acc_scr)

    acc_scr[...] += _swiglu_chunk(h_scr[...], wg_ref[...], wu_ref[...], wd_ref[...])

    @pl.when(j == pl.num_programs(1) - 1)
    def _():
        g2 = mod_ref[0, 5:6, :]
        o_ref[...] = x_ref[...] + g2 * _rms(acc_scr[...], ng_ref[3:4, :])


def _ffn(x2d, seq, mod, ng, wg, wu, wd):
    n, d = x2d.shape
    f = wg.shape[1]
    tm = min(FFN_ROWS, seq)
    fc = FFN_COLS
    assert seq % tm == 0 and f % fc == 0
    per_seq = seq // tm
    return pl.pallas_call(
        _ffn_kernel,
        grid=(n // tm, f // fc),
        in_specs=[pl.BlockSpec((tm, d), lambda i, j: (i, 0)),
                  pl.BlockSpec((1, 6, d), lambda i, j: (i // per_seq, 0, 0)),
                  pl.BlockSpec((4, d), lambda i, j: (0, 0)),
                  pl.BlockSpec((d, fc), lambda i, j: (0, j)),
                  pl.BlockSpec((d, fc), lambda i, j: (0, j)),
                  pl.BlockSpec((fc, d), lambda i, j: (j, 0))],
        out_specs=pl.BlockSpec((tm, d), lambda i, j: (i, 0)),
        out_shape=jax.ShapeDtypeStruct((n, d), F32),
        scratch_shapes=[pltpu.VMEM((tm, d), BF16), pltpu.VMEM((tm, d), F32)],
        compiler_params=_cparams(("arbitrary", "arbitrary")),
        name="ffn_dense",
    )(x2d, mod, ng, wg.astype(BF16), wu.astype(BF16), wd.astype(BF16))


def _route_kernel(x_ref, mod_ref, ng_ref, rw_ref, rb_ref, h_ref, eid_ref, gate_ref, rank_ref,
                  base_ref, cnt_ref, run_scr):
    i = pl.program_id(0)
    tr = x_ref.shape[0]

    @pl.when(i == 0)
    def _():
        run_scr[...] = jnp.zeros_like(run_scr)

    base_ref[0] = run_scr[...].astype(I32)
    sh2 = mod_ref[0, 3:4, :]
    sc2 = mod_ref[0, 4:5, :]
    h = _rms(x_ref[...], ng_ref[2:3, :]) * (1.0 + sc2) + sh2
    h_ref[...] = h.astype(BF16)
    logits = lax.dot_general(rw_ref[...], h, (((1,), (1,)), ((), ())),
                             preferred_element_type=F32,
                             precision=lax.Precision.HIGHEST) + rb_ref[...]
    e_iota = lax.broadcasted_iota(I32, logits.shape, 0)
    m1 = jnp.max(logits, axis=0, keepdims=True)
    i1 = jnp.min(jnp.where(logits == m1, e_iota, N_EXPERTS), axis=0, keepdims=True)
    oh1 = e_iota == i1
    rest = jnp.where(oh1, -jnp.inf, logits)
    m2 = jnp.max(rest, axis=0, keepdims=True)
    i2 = jnp.min(jnp.where(rest == m2, e_iota, N_EXPERTS), axis=0, keepdims=True)
    oh2 = e_iota == i2
    e2 = jnp.exp(m2 - m1)
    den = 1.0 + e2
    gate_ref[...] = jnp.concatenate([1.0 / den, e2 / den], axis=0)
    eid_ref[...] = jnp.concatenate([i1, i2], axis=0)

    member = oh1.astype(F32) + oh2.astype(F32)
    before = (lax.broadcasted_iota(I32, (tr, tr), 0) <
              lax.broadcasted_iota(I32, (tr, tr), 1)).astype(BF16)
    prefix = jnp.dot(member.astype(BF16), before, preferred_element_type=F32) + run_scr[:, 0:1]
    r1 = jnp.sum(jnp.where(oh1, prefix, 0.0), axis=0, keepdims=True)
    r2 = jnp.sum(jnp.where(oh2, prefix, 0.0), axis=0, keepdims=True)
    rank_ref[...] = jnp.concatenate([r1, r2], axis=0).astype(I32)
    run_scr[...] += jnp.sum(member, axis=1, keepdims=True)
    cnt_ref[...] = run_scr[...].astype(I32)


def _route(x2d, seq, mod, ng, router_w, router_b):
    n, d = x2d.shape
    tr = MOVE_ROWS
    assert seq % tr == 0
    per_seq = seq // tr
    return pl.pallas_call(
        _route_kernel,
        grid=(n // tr,),
        in_specs=[pl.BlockSpec((tr, d), lambda i: (i, 0)),
                  pl.BlockSpec((1, 6, d), lambda i: (i // per_seq, 0, 0)),
                  pl.BlockSpec((4, d), lambda i: (0, 0)),
                  pl.BlockSpec((N_EXPERTS, d), lambda i: (0, 0)),
                  pl.BlockSpec((N_EXPERTS, 1), lambda i: (0, 0))],
        out_specs=[pl.BlockSpec((tr, d), lambda i: (i, 0)),
                   pl.BlockSpec((TOP_K, tr), lambda i: (0, i)),
                   pl.BlockSpec((TOP_K, tr), lambda i: (0, i)),
                   pl.BlockSpec((TOP_K, tr), lambda i: (0, i)),
                   pl.BlockSpec((1, N_EXPERTS, LANES), lambda i: (i, 0, 0)),
                   pl.BlockSpec((N_EXPERTS, LANES), lambda i: (0, 0))],
        out_shape=[jax.ShapeDtypeStruct((n, d), BF16),
                   jax.ShapeDtypeStruct((TOP_K, n), I32),
                   jax.ShapeDtypeStruct((TOP_K, n), F32),
                   jax.ShapeDtypeStruct((TOP_K, n), I32),
                   jax.ShapeDtypeStruct((n // tr, N_EXPERTS, LANES), I32),
                   jax.ShapeDtypeStruct((N_EXPERTS, LANES), I32)],
        scratch_shapes=[pltpu.VMEM((N_EXPERTS, LANES), F32)],
        compiler_params=_cparams(("arbitrary",)),
        name="moe_route",
    )(x2d, mod, ng, router_w.T, router_b.reshape(N_EXPERTS, 1))


def _window_copy(src, dst, sem):
    return pltpu.make_async_copy(src, dst, sem)


def _rows_at(ref, row, n_rows):
    return ref.at[pl.ds(pl.multiple_of(row * SUBLANES, SUBLANES), n_rows * SUBLANES), :]


def _to_tiles(dst, val):
    rows = val.shape[0]
    for jj in range(val.shape[1] // LANES):
        dst[pl.ds(jj, rows, stride=SUBLANES), :] = val[:, jj * LANES:(jj + 1) * LANES]


def _from_tiles(src, rows):
    return jnp.concatenate([src[pl.ds(jj, rows, stride=SUBLANES), :] for jj in range(SUBLANES)],
                           axis=1)


def _dispatch_kernel(start_ref, count_ref, comp_ref, base_ref, ctile_ref, h_ref, eid_ref,
                     rank_ref, hs_ref, win_scr, ovf_scr, zero_scr, sems, sync_sem):
    i = pl.program_id(0)
    n_steps = pl.num_programs(0)
    slot = i % 2
    w = WINDOW
    t = h_ref.shape[0]
    h = h_ref[...]
    e0 = eid_ref[0:1, :]
    e1 = eid_ref[1:2, :]
    r0 = rank_ref[0:1, :]
    r1 = rank_ref[1:2, :]
    row = lax.broadcasted_iota(I32, (w, t), 0)

    def local_rank(e):
        return jnp.where(e0 == e, r0, jnp.where(e1 == e, r1, -1)) - base_ref[i * N_EXPERTS + e]

    def select(lr, first_row):
        p = jnp.where(row + first_row == lr, 1.0, 0.0).astype(BF16)
        return jnp.dot(p, h, preferred_element_type=F32)

    for e in range(N_EXPERTS):
        _to_tiles(win_scr.at[slot, e], select(local_rank(e), 0))

    def window(step_slot, e, step):
        dst_row = start_ref[e] + base_ref[step * N_EXPERTS + e]
        return _window_copy(win_scr.at[step_slot, e], _rows_at(hs_ref, dst_row, w),
                            sems.at[step_slot])

    @pl.when(i > 0)
    def _():
        for e in range(N_EXPERTS):
            window(1 - slot, e, i - 1).wait()

    for e in range(N_EXPERTS):
        window(slot, e, i).start()

    for e in range(N_EXPERTS):
        @pl.when(ctile_ref[i * N_EXPERTS + e] > w)
        def _():
            _to_tiles(ovf_scr, select(local_rank(e), w))
            dst_row = start_ref[e] + base_ref[i * N_EXPERTS + e] + w
            cp = _window_copy(ovf_scr, _rows_at(hs_ref, dst_row, w), sync_sem)
            cp.start()
            cp.wait()

    @pl.when(i == n_steps - 1)
    def _():
        for e in range(N_EXPERTS):
            window(slot, e, i).wait()
        zero_scr[...] = jnp.zeros_like(zero_scr)
        tm = zero_scr.shape[0] // SUBLANES

        def zero_fill(row_start):
            cp = _window_copy(zero_scr, _rows_at(hs_ref, row_start, tm), sync_sem)
            cp.start()
            cp.wait()

        for e in range(N_EXPERTS):
            zero_fill(start_ref[e] + count_ref[e])
            zero_fill(start_ref[e] + comp_ref[e])
        used = start_ref[N_EXPERTS - 1] + comp_ref[N_EXPERTS - 1] + tm
        total = hs_ref.shape[0] // SUBLANES

        def tail(k, c):
            zero_fill(used + k * tm)
            return c

        lax.fori_loop(0, (total - used) // tm, tail, 0)


def _dispatch(scalars, h_bf, eid, rank, n_rows):
    n, d = h_bf.shape
    t = MOVE_ROWS
    w = WINDOW
    assert n % t == 0 and d == SUBLANES * LANES and t <= 2 * w
    grid_spec = pltpu.PrefetchScalarGridSpec(
        num_scalar_prefetch=5,
        grid=(n // t,),
        in_specs=[pl.BlockSpec((t, d), lambda i, *_: (i, 0)),
                  pl.BlockSpec((TOP_K, t), lambda i, *_: (0, i)),
                  pl.BlockSpec((TOP_K, t), lambda i, *_: (0, i))],
        out_specs=pl.BlockSpec(memory_space=pl.ANY),
        scratch_shapes=[pltpu.VMEM((2, N_EXPERTS, w * SUBLANES, LANES), F32),
                        pltpu.VMEM((w * SUBLANES, LANES), F32),
                        pltpu.VMEM((MOE_ROWS * SUBLANES, LANES), F32),
                        pltpu.SemaphoreType.DMA((2,)),
                        pltpu.SemaphoreType.DMA],
    )
    return pl.pallas_call(
        _dispatch_kernel,
        grid_spec=grid_spec,
        out_shape=jax.ShapeDtypeStruct((n_rows * SUBLANES, LANES), F32),
        compiler_params=_cparams(("arbitrary",)),
        name="moe_dispatch",
    )(*scalars, h_bf, eid, rank)


def _expert_kernel(te_ref, tv_ref, hs_ref, wg_ref, wu_ref, wd_ref, y_ref, h_scr, acc_scr):
    del te_ref
    i = pl.program_id(0)
    j = pl.program_id(1)
    last = pl.num_programs(1) - 1
    tm = h_scr.shape[0]

    @pl.when(tv_ref[i] > 0)
    def _():
        @pl.when(j == 0)
        def _():
            h_scr[...] = _from_tiles(hs_ref, tm).astype(BF16)
            acc_scr[...] = jnp.zeros_like(acc_scr)

        acc_scr[...] += _swiglu_chunk(h_scr[...], wg_ref[0], wu_ref[0], wd_ref[0])

        @pl.when(j == last)
        def _():
            _to_tiles(y_ref, acc_scr[...])

    @pl.when(jnp.logical_and(tv_ref[i] == 0, j == last))
    def _():
        y_ref[...] = jnp.zeros_like(y_ref)


def _experts(tile_expert, tile_valid, hs, wg, wu, wd):
    d = wg.shape[1]
    f = wg.shape[2]
    tm = MOE_ROWS
    fc = FFN_COLS
    n_tiles = hs.shape[0] // (tm * SUBLANES)
    n_fc = f // fc

    def col(i, j, tv):
        return jnp.where(tv[i] > 0, j, n_fc - 1)

    grid_spec = pltpu.PrefetchScalarGridSpec(
        num_scalar_prefetch=2,
        grid=(n_tiles, n_fc),
        in_specs=[pl.BlockSpec((tm * SUBLANES, LANES), lambda i, j, te, tv: (i, 0)),
                  pl.BlockSpec((1, d, fc), lambda i, j, te, tv: (te[i], 0, col(i, j, tv))),
                  pl.BlockSpec((1, d, fc), lambda i, j, te, tv: (te[i], 0, col(i, j, tv))),
                  pl.BlockSpec((1, fc, d), lambda i, j, te, tv: (te[i], col(i, j, tv), 0))],
        out_specs=pl.BlockSpec((tm * SUBLANES, LANES), lambda i, j, te, tv: (i, 0)),
        scratch_shapes=[pltpu.VMEM((tm, d), BF16), pltpu.VMEM((tm, d), F32)],
    )
    return pl.pallas_call(
        _expert_kernel,
        grid_spec=grid_spec,
        out_shape=jax.ShapeDtypeStruct(hs.shape, F32),
        compiler_params=_cparams(("arbitrary", "arbitrary")),
        name="moe_experts",
    )(tile_expert, tile_valid, hs, wg.astype(BF16), wu.astype(BF16), wd.astype(BF16))


def _split(v):
    hi = v.astype(BF16)
    lo = (v - hi.astype(F32)).astype(BF16)
    return hi, lo


def _dot3(q, y):
    qh, ql = _split(q)
    yh, yl = _split(y)
    return (jnp.dot(qh, yh, preferred_element_type=F32)
            + jnp.dot(qh, yl, preferred_element_type=F32)
            + jnp.dot(ql, yh, preferred_element_type=F32))


def _combine_kernel(start_ref, comp_ref, base_ref, ctile_ref, eid_ref, rank_ref, gate_ref, x_ref,
                    mod_ref, ng_ref, y_ref, o_ref, win_scr, ovf_scr, f_scr, sems, sync_sem):
    i = pl.program_id(0)
    n_steps = pl.num_programs(0)
    slot = i % 2
    w = WINDOW
    t = x_ref.shape[0]
    e0 = eid_ref[:, 0:1]
    e1 = eid_ref[:, 1:2]
    r0 = rank_ref[:, 0:1]
    r1 = rank_ref[:, 1:2]
    g0 = gate_ref[:, 0:1]
    g1 = gate_ref[:, 1:2]
    col = lax.broadcasted_iota(I32, (t, w), 1)

    def first_row(step, e, second):
        want = base_ref[step * N_EXPERTS + e] + (w if second else 0)
        return start_ref[e] + jnp.minimum(want, comp_ref[e] - w)

    def window(step_slot, e, step):
        return _window_copy(_rows_at(y_ref, first_row(step, e, False), w),
                            win_scr.at[step_slot, e], sems.at[step_slot])

    @pl.when(i == 0)
    def _():
        for e in range(N_EXPERTS):
            window(slot, e, i).start()

    @pl.when(i + 1 < n_steps)
    def _():
        for e in range(N_EXPERTS):
            window(1 - slot, e, i + 1).start()

    for e in range(N_EXPERTS):
        window(slot, e, i).wait()

    def weights(e, second):
        mine0 = e0 == e
        mine1 = e1 == e
        rank = jnp.where(mine0, r0, jnp.where(mine1, r1, -1))
        gate = jnp.where(mine0, g0, jnp.where(mine1, g1, 0.0))
        local = rank - base_ref[i * N_EXPERTS + e]
        in_window = (local >= w) if second else jnp.logical_and(local >= 0, local < w)
        pos = rank + start_ref[e] - first_row(i, e, second)
        return jnp.where(jnp.logical_and(in_window, pos == col), gate, 0.0)

    f = jnp.zeros((t, x_ref.shape[1]), F32)
    for e in range(0, N_EXPERTS, 2):
        q = jnp.concatenate([weights(e, False), weights(e + 1, False)], axis=1)
        y = jnp.concatenate([_from_tiles(win_scr.at[slot, e], w),
                             _from_tiles(win_scr.at[slot, e + 1], w)], axis=0)
        f = f + _dot3(q, y)
    f_scr[...] = f

    for e in range(N_EXPERTS):
        @pl.when(ctile_ref[i * N_EXPERTS + e] > w)
        def _():
            cp = _window_copy(_rows_at(y_ref, first_row(i, e, True), w), ovf_scr, sync_sem)
            cp.start()
            cp.wait()
            f_scr[...] += _dot3(weights(e, True), _from_tiles(ovf_scr, w))

    g2 = mod_ref[0, 5:6, :]
    o_ref[...] = x_ref[...] + g2 * _rms(f_scr[...], ng_ref[3:4, :])


def _combine(scalars, eid_c, rank_c, gate_c, x2d, seq, mod, ng, y):
    n, d = x2d.shape
    t = MOVE_ROWS
    w = WINDOW
    assert seq % t == 0 and t <= 2 * w
    per_seq = seq // t
    grid_spec = pltpu.PrefetchScalarGridSpec(
        num_scalar_prefetch=4,
        grid=(n // t,),
        in_specs=[pl.BlockSpec((t, TOP_K), lambda i, *_: (i, 0)),
                  pl.BlockSpec((t, TOP_K), lambda i, *_: (i, 0)),
                  pl.BlockSpec((t, TOP_K), lambda i, *_: (i, 0)),
                  pl.BlockSpec((t, d), lambda i, *_: (i, 0)),
                  pl.BlockSpec((1, 6, d), lambda i, *_: (i // per_seq, 0, 0)),
                  pl.BlockSpec((4, d), lambda i, *_: (0, 0)),
                  pl.BlockSpec(memory_space=pl.ANY)],
        out_specs=pl.BlockSpec((t, d), lambda i, *_: (i, 0)),
        scratch_shapes=[pltpu.VMEM((2, N_EXPERTS, w * SUBLANES, LANES), F32),
                        pltpu.VMEM((w * SUBLANES, LANES), F32),
                        pltpu.VMEM((t, d), F32),
                        pltpu.SemaphoreType.DMA((2,)),
                        pltpu.SemaphoreType.DMA],
    )
    return pl.pallas_call(
        _combine_kernel,
        grid_spec=grid_spec,
        out_shape=jax.ShapeDtypeStruct((n, d), F32),
        compiler_params=_cparams(("arbitrary",)),
        name="moe_combine",
    )(*scalars, eid_c, rank_c, gate_c, x2d, mod, ng, y)


def _moe(x2d, seq, mod, ng, router_w, router_b, wg, wu, wd):
    n, d = x2d.shape
    tm = MOE_ROWS
    h_bf, eid, gate, rank, base, cnt = _route(x2d, seq, mod, ng, router_w, router_b)
    counts = cnt[:, 0]
    comp = jnp.maximum(((counts + tm - 1) // tm) * tm, tm)
    ends = jnp.cumsum(comp + tm)
    starts = ends - (comp + tm)
    n_tiles = (TOP_K * n) // tm + 2 * N_EXPERTS
    tile_start = jnp.arange(n_tiles, dtype=I32) * tm
    tile_expert = jnp.minimum(jnp.sum(tile_start[:, None] >= ends[None, :], axis=1),
                              N_EXPERTS - 1).astype(I32)
    tile_valid = jnp.logical_and(tile_start < (starts + comp)[tile_expert],
                                 tile_start < ends[-1]).astype(I32)
    base = base[:, :, 0]
    ctile = jnp.concatenate([base[1:], counts[None, :]], axis=0) - base
    base = base.reshape(-1)
    ctile = ctile.reshape(-1)
    hs = _dispatch((starts, counts, comp, base, ctile), h_bf, eid, rank, n_tiles * tm)
    y = _experts(tile_expert, tile_valid, hs, wg, wu, wd)
    return _combine((starts, comp, base, ctile), eid.T, rank.T, gate.T, x2d, seq, mod, ng, y)


def kernel(x, c, w_ada, b_ada, norm_gain, w_in, b_in, ln_v_gain, ln_v_bias, w_spatial, b_spatial, conv_w, conv_b, ln_conv_gain, ln_conv_bias, group_gain, w_out, ffn_w_gate, ffn_w_up, ffn_w_down, router_w, router_b, moe_w_gate, moe_w_up, moe_w_down):
    bsz, seq, d = x.shape
    depth = w_ada.shape[0]
    mod_all = _ada(c, w_ada, b_ada).reshape(depth, bsz, 6, d)
    for l in range(depth):
        mod = mod_all[l]
        ng = norm_gain[l]
        x = _mix(x, mod, ng, w_in[l], b_in[l], ln_v_gain[l], ln_v_bias[l], w_spatial[l],
                 b_spatial[l], conv_w[l], conv_b[l], ln_conv_gain[l], ln_conv_bias[l],
                 group_gain[l], w_out[l])
        x2d = x.reshape(bsz * seq, d)
        i = l // 2
        if l % 2 == 0:
            x2d = _ffn(x2d, seq, mod, ng, ffn_w_gate[i], ffn_w_up[i], ffn_w_down[i])
        else:
            x2d = _moe(x2d, seq, mod, ng, router_w[i], router_b[i], moe_w_gate[i],
                       moe_w_up[i], moe_w_down[i])
        x = x2d.reshape(bsz, seq, d)
    return x
```

```python
import jax
import jax.numpy as jnp
from jax import lax
from jax.experimental import pallas as pl
from jax.experimental.pallas import tpu as pltpu

F32 = jnp.float32
BF16 = jnp.bfloat16
I32 = jnp.int32

EPS = 1e-6
CHUNK = 64
GMLP_BLOCK = 128
N_HEADS_A = 8
CONV_WIDTH = 31
N_EXPERTS = 8
TOP_K = 2

LANES = 128
SUBLANES = 8
CONV_HALO = 32
VMEM_LIMIT = 56 * 1024 * 1024

SEQ_TILE = 512
SUB_TILE = 256
FFN_ROWS = 512
FFN_COLS = 1792
MOE_ROWS = 512
MOVE_ROWS = 256
WINDOW = MOVE_ROWS // 2


def _rms(x, g):
    return x * lax.rsqrt(jnp.mean(x * x, axis=-1, keepdims=True) + EPS) * g


def _layer_norm(x, g, b):
    mu = jnp.mean(x, axis=-1, keepdims=True)
    xc = x - mu
    return xc * lax.rsqrt(jnp.mean(xc * xc, axis=-1, keepdims=True) + EPS) * g + b


_SQRT_2_OVER_PI = 0.7978845608028654


def _sigmoid(x):
    return 0.5 + 0.5 * jnp.tanh(0.5 * x)


def _silu(x):
    return x * _sigmoid(x)


def _gelu(x):
    inner = x * (_SQRT_2_OVER_PI + (_SQRT_2_OVER_PI * 0.044715) * (x * x))
    half = 0.5 * x
    return half + half * jnp.tanh(inner)


def _cparams(sem, vmem=VMEM_LIMIT):
    return pltpu.CompilerParams(dimension_semantics=sem, vmem_limit_bytes=vmem)


def _ada_kernel(c_ref, w_ref, b_ref, o_ref):
    c_act = jax.nn.silu(c_ref[...])
    o_ref[0] = jnp.dot(c_act, w_ref[0], preferred_element_type=F32,
                       precision=lax.Precision.HIGHEST) + b_ref[0]


def _ada(c, w_ada, b_ada):
    depth, d, n6 = w_ada.shape
    bsz = c.shape[0]
    nc = 1536
    return pl.pallas_call(
        _ada_kernel,
        grid=(depth, n6 // nc),
        in_specs=[pl.BlockSpec((bsz, d), lambda l, j: (0, 0)),
                  pl.BlockSpec((1, d, nc), lambda l, j: (l, 0, j)),
                  pl.BlockSpec((1, 1, nc), lambda l, j: (l, 0, j))],
        out_specs=pl.BlockSpec((1, bsz, nc), lambda l, j: (l, 0, j)),
        out_shape=jax.ShapeDtypeStruct((depth, bsz, n6), F32),
        compiler_params=_cparams(("arbitrary", "arbitrary")),
        name="ada_mod",
    )(c, w_ada, b_ada.reshape(depth, 1, n6))


def _mix_kernel(x_ref, mod_ref, ng_ref, w_in_ref, b_in_ref, lnv_g_ref, lnv_b_ref, ws_ref,
                bs_ref, cw_ref, cb_ref, lnc_g_ref, lnc_b_ref, gg_ref, w_out_ref,
                o_ref, wsp_scr, xg_scr, sh_scr, yc_scr):
    ts = x_ref.shape[1]
    d_a = lnv_g_ref.shape[1]
    d_b = lnc_g_ref.shape[1]
    b = pl.program_id(0)
    s = pl.program_id(1)

    @pl.when(jnp.logical_and(b == 0, s == 0))
    def _():
        t_chunk = lax.broadcasted_iota(I32, (GMLP_BLOCK, GMLP_BLOCK), 0) // CHUNK
        s_chunk = lax.broadcasted_iota(I32, (GMLP_BLOCK, GMLP_BLOCK), 1) // CHUNK
        allowed = t_chunk >= s_chunk
        for j in range(N_HEADS_A // 2):
            lo = jnp.where(allowed, ws_ref[2 * j], 0.0).astype(BF16)
            hi = jnp.where(allowed, ws_ref[2 * j + 1], 0.0).astype(BF16)
            wsp_scr[j] = jnp.concatenate([lo, hi], axis=1)

    @pl.when(s == 0)
    def _():
        xg_scr[0:CONV_HALO, :] = jnp.zeros((CONV_HALO, d_b), F32)

    sh1 = mod_ref[0, 0:1, :]
    sc1 = mod_ref[0, 1:2, :]
    g1 = mod_ref[0, 2:3, :]
    head_dim = d_a // N_HEADS_A
    lane = lax.broadcasted_iota(I32, (GMLP_BLOCK, LANES), 1)
    first_head = lane < head_dim
    zero = jnp.zeros((GMLP_BLOCK, LANES), BF16)
    first_tap = CONV_HALO - (CONV_WIDTH - 1)
    keep = CONV_HALO - SUBLANES
    rows = 64
    sub = min(SUB_TILE, ts)

    for q in range(ts // sub):
        lo = q * sub
        x = x_ref[0, lo:lo + sub, :]
        h = _rms(x, ng_ref[0:1, :]) * (1.0 + sc1) + sh1
        z = jnp.dot(h.astype(BF16), w_in_ref[...], preferred_element_type=F32) + b_in_ref[...]
        ua = z[:, 0:d_a]
        va = z[:, d_a:2 * d_a]
        ab = z[:, 2 * d_a:2 * d_a + d_b]
        gb = z[:, 2 * d_a + d_b:]

        u = _gelu(ua)
        v = _layer_norm(_gelu(va), lnv_g_ref[...], lnv_b_ref[...]).astype(BF16)
        blocks = []
        for n in range(sub // GMLP_BLOCK):
            cols = []
            for j in range(d_a // LANES):
                vc = v[n * GMLP_BLOCK:(n + 1) * GMLP_BLOCK, j * LANES:(j + 1) * LANES]
                rhs = jnp.concatenate([jnp.where(first_head, vc, zero),
                                       jnp.where(first_head, zero, vc)], axis=0)
                cols.append(jnp.dot(wsp_scr[j], rhs, preferred_element_type=F32))
            blocks.append(jnp.concatenate(cols, axis=1) + bs_ref[...])
        ya = u * jnp.concatenate(blocks, axis=0)

        xg_scr[CONV_HALO + lo:CONV_HALO + lo + sub, :] = ab * _sigmoid(gb)
        new_lo = lo if q == 0 else lo + keep
        new_hi = lo + sub + keep
        for r in range(1, SUBLANES):
            sh_scr[r, new_lo:new_hi, :] = xg_scr[new_lo + r:new_hi + r, :]
        for rc in range(sub // rows):
            base = lo + rc * rows
            for lc in range(d_b // LANES):
                ls = slice(lc * LANES, (lc + 1) * LANES)
                acc = jnp.broadcast_to(cb_ref[:, ls], (rows, LANES))
                for k in range(CONV_WIDTH):
                    off = first_tap + k
                    r = off % SUBLANES
                    r0 = base + off - r
                    if r == 0:
                        win = xg_scr[r0:r0 + rows, ls]
                    else:
                        win = sh_scr[r, r0:r0 + rows, ls]
                    acc = acc + cw_ref[k:k + 1, ls] * win
                yc_scr[base:base + rows, ls] = acc
        yb = _silu(_layer_norm(yc_scr[lo:lo + sub, :], lnc_g_ref[...], lnc_b_ref[...]))

        ycat = jnp.concatenate([_rms(ya, gg_ref[:, 0:d_a]), _rms(yb, gg_ref[:, d_a:])], axis=1)
        y = jnp.dot(ycat.astype(BF16), w_out_ref[...], preferred_element_type=F32)
        o_ref[0, lo:lo + sub, :] = x + g1 * _rms(y, ng_ref[1:2, :])

    xg_scr[0:CONV_HALO, :] = xg_scr[ts:ts + CONV_HALO, :]


def _mix(x, mod, ng, w_in, b_in, lnv_g, lnv_b, w_sp, b_sp, cw, cb, lnc_g, lnc_b, gg, w_out):
    bsz, seq, d = x.shape
    d_in = w_in.shape[1]
    d_a = lnv_g.shape[0]
    d_b = lnc_g.shape[0]
    ts = min(SEQ_TILE, seq)
    assert seq % ts == 0 and ts % GMLP_BLOCK == 0 and ts >= CONV_HALO
    bs_full = jnp.repeat(b_sp.T, d_a // N_HEADS_A, axis=1)
    const = lambda *shape: pl.BlockSpec(shape, lambda b, s: (0,) * len(shape))
    return pl.pallas_call(
        _mix_kernel,
        grid=(bsz, seq // ts),
        in_specs=[pl.BlockSpec((1, ts, d), lambda b, s: (b, s, 0)),
                  pl.BlockSpec((1, 6, d), lambda b, s: (b, 0, 0)),
                  const(4, d), const(d, d_in), const(1, d_in), const(1, d_a), const(1, d_a),
                  const(N_HEADS_A, GMLP_BLOCK, GMLP_BLOCK), const(GMLP_BLOCK, d_a),
                  const(CONV_WIDTH, d_b), const(1, d_b), const(1, d_b), const(1, d_b),
                  const(1, d_a + d_b), const(d_a + d_b, d)],
        out_specs=pl.BlockSpec((1, ts, d), lambda b, s: (b, s, 0)),
        out_shape=jax.ShapeDtypeStruct((bsz, seq, d), F32),
        scratch_shapes=[pltpu.VMEM((N_HEADS_A // 2, GMLP_BLOCK, 2 * GMLP_BLOCK), BF16),
                        pltpu.VMEM((ts + CONV_HALO, d_b), F32),
                        pltpu.VMEM((SUBLANES, ts + CONV_HALO, d_b), F32),
                        pltpu.VMEM((ts, d_b), F32)],
        compiler_params=_cparams(("arbitrary", "arbitrary")),
        name="token_mix",
    )(x, mod, ng, w_in.astype(BF16), b_in.reshape(1, d_in), lnv_g.reshape(1, d_a),
      lnv_b.reshape(1, d_a), w_sp, bs_full, cw, cb.reshape(1, d_b), lnc_g.reshape(1, d_b),
      lnc_b.reshape(1, d_b), gg.reshape(1, d_a + d_b), w_out.astype(BF16))


def _swiglu_chunk(h, wg, wu, wd):
    g = jnp.dot(h, wg, preferred_element_type=F32)
    u = jnp.dot(h, wu, preferred_element_type=F32)
    a = (_silu(g) * u).astype(BF16)
    return jnp.dot(a, wd, preferred_element_type=F32)


def _ffn_kernel(x_ref, mod_ref, ng_ref, wg_ref, wu_ref, wd_ref, o_ref, h_scr, acc_scr):
    j = pl.program_id(1)

    @pl.when(j == 0)
    def _():
        sh2 = mod_ref[0, 3:4, :]
        sc2 = mod_ref[0, 4:5, :]
        h = _rms(x_ref[...], ng_ref[2:3, :]) * (1.0 + sc2) + sh2
        h_scr[...] = h.astype(BF16)
        acc_scr[...] = jnp.zeros_like(acc_scr)

    acc_scr[...] += _swiglu_chunk(h_scr[...], wg_ref[...], wu_ref[...], wd_ref[...])

    @pl.when(j == pl.num_programs(1) - 1)
    def _():
        g2 = mod_ref[0, 5:6, :]
        o_ref[...] = x_ref[...] + g2 * _rms(acc_scr[...], ng_ref[3:4, :])


def _ffn(x2d, seq, mod, ng, wg, wu, wd):
    n, d = x2d.shape
    f = wg.shape[1]
    tm = min(FFN_ROWS, seq)
    fc = FFN_COLS
    assert seq % tm == 0 and f % fc == 0
    per_seq = seq // tm
    return pl.pallas_call(
        _ffn_kernel,
        grid=(n // tm, f // fc),
        in_specs=[pl.BlockSpec((tm, d), lambda i, j: (i, 0)),
                  pl.BlockSpec((1, 6, d), lambda i, j: (i // per_seq, 0, 0)),
                  pl.BlockSpec((4, d), lambda i, j: (0, 0)),
                  pl.BlockSpec((d, fc), lambda i, j: (0, j)),
                  pl.BlockSpec((d, fc), lambda i, j: (0, j)),
                  pl.BlockSpec((fc, d), lambda i, j: (j, 0))],
        out_specs=pl.BlockSpec((tm, d), lambda i, j: (i, 0)),
        out_shape=jax.ShapeDtypeStruct((n, d), F32),
        scratch_shapes=[pltpu.VMEM((tm, d), BF16), pltpu.VMEM((tm, d), F32)],
        compiler_params=_cparams(("arbitrary", "arbitrary")),
        name="ffn_dense",
    )(x2d, mod, ng, wg.astype(BF16), wu.astype(BF16), wd.astype(BF16))


def _route_kernel(x_ref, mod_ref, ng_ref, rw_ref, rb_ref, h_ref, eid_ref, gate_ref, rank_ref,
                  base_ref, cnt_ref, run_scr):
    i = pl.program_id(0)
    tr = x_ref.shape[0]

    @pl.when(i == 0)
    def _():
        run_scr[...] = jnp.zeros_like(run_scr)

    base_ref[0] = run_scr[...].astype(I32)
    sh2 = mod_ref[0, 3:4, :]
    sc2 = mod_ref[0, 4:5, :]
    h = _rms(x_ref[...], ng_ref[2:3, :]) * (1.0 + sc2) + sh2
    h_ref[...] = h.astype(BF16)
    logits = lax.dot_general(rw_ref[...], h, (((1,), (1,)), ((), ())),
                             preferred_element_type=F32,
                             precision=lax.Precision.HIGHEST) + rb_ref[...]
    e_iota = lax.broadcasted_iota(I32, logits.shape, 0)
    m1 = jnp.max(logits, axis=0, keepdims=True)
    i1 = jnp.min(jnp.where(logits == m1, e_iota, N_EXPERTS), axis=0, keepdims=True)
    oh1 = e_iota == i1
    rest = jnp.where(oh1, -jnp.inf, logits)
    m2 = jnp.max(rest, axis=0, keepdims=True)
    i2 = jnp.min(jnp.where(rest == m2, e_iota, N_EXPERTS), axis=0, keepdims=True)
    oh2 = e_iota == i2
    e2 = jnp.exp(m2 - m1)
    den = 1.0 + e2
    gate_ref[...] = jnp.concatenate([1.0 / den, e2 / den], axis=0)
    eid_ref[...] = jnp.concatenate([i1, i2], axis=0)

    member = oh1.astype(F32) + oh2.astype(F32)
    before = (lax.broadcasted_iota(I32, (tr, tr), 0) <
              lax.broadcasted_iota(I32, (tr, tr), 1)).astype(BF16)
    prefix = jnp.dot(member.astype(BF16), before, preferred_element_type=F32) + run_scr[:, 0:1]
    r1 = jnp.sum(jnp.where(oh1, prefix, 0.0), axis=0, keepdims=True)
    r2 = jnp.sum(jnp.where(oh2, prefix, 0.0), axis=0, keepdims=True)
    rank_ref[...] = jnp.concatenate([r1, r2], axis=0).astype(I32)
    run_scr[...] += jnp.sum(member, axis=1, keepdims=True)
    cnt_ref[...] = run_scr[...].astype(I32)


def _route(x2d, seq, mod, ng, router_w, router_b):
    n, d = x2d.shape
    tr = MOVE_ROWS
    assert seq % tr == 0
    per_seq = seq // tr
    return pl.pallas_call(
        _route_kernel,
        grid=(n // tr,),
        in_specs=[pl.BlockSpec((tr, d), lambda i: (i, 0)),
                  pl.BlockSpec((1, 6, d), lambda i: (i // per_seq, 0, 0)),
                  pl.BlockSpec((4, d), lambda i: (0, 0)),
                  pl.BlockSpec((N_EXPERTS, d), lambda i: (0, 0)),
                  pl.BlockSpec((N_EXPERTS, 1), lambda i: (0, 0))],
        out_specs=[pl.BlockSpec((tr, d), lambda i: (i, 0)),
                   pl.BlockSpec((TOP_K, tr), lambda i: (0, i)),
                   pl.BlockSpec((TOP_K, tr), lambda i: (0, i)),
                   pl.BlockSpec((TOP_K, tr), lambda i: (0, i)),
                   pl.BlockSpec((1, N_EXPERTS, LANES), lambda i: (i, 0, 0)),
                   pl.BlockSpec((N_EXPERTS, LANES), lambda i: (0, 0))],
        out_shape=[jax.ShapeDtypeStruct((n, d), BF16),
                   jax.ShapeDtypeStruct((TOP_K, n), I32),
                   jax.ShapeDtypeStruct((TOP_K, n), F32),
                   jax.ShapeDtypeStruct((TOP_K, n), I32),
                   jax.ShapeDtypeStruct((n // tr, N_EXPERTS, LANES), I32),
                   jax.ShapeDtypeStruct((N_EXPERTS, LANES), I32)],
        scratch_shapes=[pltpu.VMEM((N_EXPERTS, LANES), F32)],
        compiler_params=_cparams(("arbitrary",)),
        name="moe_route",
    )(x2d, mod, ng, router_w.T, router_b.reshape(N_EXPERTS, 1))


def _window_copy(src, dst, sem):
    return pltpu.make_async_copy(src, dst, sem)


def _rows_at(ref, row, n_rows):
    return ref.at[pl.ds(pl.multiple_of(row * SUBLANES, SUBLANES), n_rows * SUBLANES), :]


def _to_tiles(dst, val):
    rows = val.shape[0]
    for jj in range(val.shape[1] // LANES):
        dst[pl.ds(jj, rows, stride=SUBLANES), :] = val[:, jj * LANES:(jj + 1) * LANES]


def _from_tiles(src, rows):
    return jnp.concatenate([src[pl.ds(jj, rows, stride=SUBLANES), :] for jj in range(SUBLANES)],
                           axis=1)


def _dispatch_kernel(start_ref, count_ref, comp_ref, base_ref, ctile_ref, h_ref, eid_ref,
                     rank_ref, hs_ref, win_scr, ovf_scr, zero_scr, sems, sync_sem):
    i = pl.program_id(0)
    n_steps = pl.num_programs(0)
    slot = i % 2
    w = WINDOW
    t = h_ref.shape[0]
    h = h_ref[...]
    e0 = eid_ref[0:1, :]
    e1 = eid_ref[1:2, :]
    r0 = rank_ref[0:1, :]
    r1 = rank_ref[1:2, :]
    row = lax.broadcasted_iota(I32, (w, t), 0)

    def local_rank(e):
        return jnp.where(e0 == e, r0, jnp.where(e1 == e, r1, -1)) - base_ref[i * N_EXPERTS + e]

    def select(lr, first_row):
        p = jnp.where(row + first_row == lr, 1.0, 0.0).astype(BF16)
        return jnp.dot(p, h, preferred_element_type=F32)

    for e in range(N_EXPERTS):
        _to_tiles(win_scr.at[slot, e], select(local_rank(e), 0))

    def window(step_slot, e, step):
        dst_row = start_ref[e] + base_ref[step * N_EXPERTS + e]
        return _window_copy(win_scr.at[step_slot, e], _rows_at(hs_ref, dst_row, w),
                            sems.at[step_slot])

    @pl.when(i > 0)
    def _():
        for e in range(N_EXPERTS):
            window(1 - slot, e, i - 1).wait()

    for e in range(N_EXPERTS):
        window(slot, e, i).start()

    for e in range(N_EXPERTS):
        @pl.when(ctile_ref[i * N_EXPERTS + e] > w)
        def _():
            _to_tiles(ovf_scr, select(local_rank(e), w))
            dst_row = start_ref[e] + base_ref[i * N_EXPERTS + e] + w
            cp = _window_copy(ovf_scr, _rows_at(hs_ref, dst_row, w), sync_sem)
            cp.start()
            cp.wait()

    @pl.when(i == n_steps - 1)
    def _():
        for e in range(N_EXPERTS):
            window(slot, e, i).wait()
        zero_scr[...] = jnp.zeros_like(zero_scr)
        tm = zero_scr.shape[0] // SUBLANES

        def zero_fill(row_start):
            cp = _window_copy(zero_scr, _rows_at(hs_ref, row_start, tm), sync_sem)
            cp.start()
            cp.wait()

        for e in range(N_EXPERTS):
            zero_fill(start_ref[e] + count_ref[e])
            zero_fill(start_ref[e] + comp_ref[e])
        used = start_ref[N_EXPERTS - 1] + comp_ref[N_EXPERTS - 1] + tm
        total = hs_ref.shape[0] // SUBLANES

        def tail(k, c):
            zero_fill(used + k * tm)
            return c

        lax.fori_loop(0, (total - used) // tm, tail, 0)


def _dispatch(scalars, h_bf, eid, rank, n_rows):
    n, d = h_bf.shape
    t = MOVE_ROWS
    w = WINDOW
    assert n % t == 0 and d == SUBLANES * LANES and t <= 2 * w
    grid_spec = pltpu.PrefetchScalarGridSpec(
        num_scalar_prefetch=5,
        grid=(n // t,),
        in_specs=[pl.BlockSpec((t, d), lambda i, *_: (i, 0)),
                  pl.BlockSpec((TOP_K, t), lambda i, *_: (0, i)),
                  pl.BlockSpec((TOP_K, t), lambda i, *_: (0, i))],
        out_specs=pl.BlockSpec(memory_space=pl.ANY),
        scratch_shapes=[pltpu.VMEM((2, N_EXPERTS, w * SUBLANES, LANES), F32),
                        pltpu.VMEM((w * SUBLANES, LANES), F32),
                        pltpu.VMEM((MOE_ROWS * SUBLANES, LANES), F32),
                        pltpu.SemaphoreType.DMA((2,)),
                        pltpu.SemaphoreType.DMA],
    )
    return pl.pallas_call(
        _dispatch_kernel,
        grid_spec=grid_spec,
        out_shape=jax.ShapeDtypeStruct((n_rows * SUBLANES, LANES), F32),
        compiler_params=_cparams(("arbitrary",)),
        name="moe_dispatch",
    )(*scalars, h_bf, eid, rank)


def _expert_kernel(te_ref, tv_ref, tb_ref, hs_ref, wg_ref, wu_ref, wd_ref, y_ref, h_scr, acc_scr):
    del te_ref, tb_ref
    i = pl.program_id(0)
    j = pl.program_id(1)
    last = pl.num_programs(1) - 1
    tm = h_scr.shape[0]

    @pl.when(tv_ref[i] > 0)
    def _():
        @pl.when(j == 0)
        def _():
            h_scr[...] = _from_tiles(hs_ref, tm).astype(BF16)
            acc_scr[...] = jnp.zeros_like(acc_scr)

        acc_scr[...] += _swiglu_chunk(h_scr[...], wg_ref[0], wu_ref[0], wd_ref[0])

        @pl.when(j == last)
        def _():
            _to_tiles(y_ref, acc_scr[...])

    @pl.when(jnp.logical_and(tv_ref[i] == 0, j == last))
    def _():
        y_ref[...] = jnp.zeros_like(y_ref)


def _experts(tile_expert, tile_valid, tile_block, hs, wg, wu, wd):
    d = wg.shape[1]
    f = wg.shape[2]
    tm = MOE_ROWS
    fc = FFN_COLS
    n_tiles = hs.shape[0] // (tm * SUBLANES)
    n_fc = f // fc

    def col(i, j, tv):
        return jnp.where(tv[i] > 0, j, n_fc - 1)

    grid_spec = pltpu.PrefetchScalarGridSpec(
        num_scalar_prefetch=3,
        grid=(n_tiles, n_fc),
        in_specs=[pl.BlockSpec((tm * SUBLANES, LANES), lambda i, j, te, tv, tb: (tb[i], 0)),
                  pl.BlockSpec((1, d, fc), lambda i, j, te, tv, tb: (te[i], 0, col(i, j, tv))),
                  pl.BlockSpec((1, d, fc), lambda i, j, te, tv, tb: (te[i], 0, col(i, j, tv))),
                  pl.BlockSpec((1, fc, d), lambda i, j, te, tv, tb: (te[i], col(i, j, tv), 0))],
        out_specs=pl.BlockSpec((tm * SUBLANES, LANES), lambda i, j, te, tv, tb: (i, 0)),
        scratch_shapes=[pltpu.VMEM((tm, d), BF16), pltpu.VMEM((tm, d), F32)],
    )
    return pl.pallas_call(
        _expert_kernel,
        grid_spec=grid_spec,
        out_shape=jax.ShapeDtypeStruct(hs.shape, F32),
        compiler_params=_cparams(("arbitrary", "arbitrary")),
        name="moe_experts",
    )(tile_expert, tile_valid, tile_block, hs, wg.astype(BF16), wu.astype(BF16), wd.astype(BF16))


def _split(v):
    hi = v.astype(BF16)
    lo = (v - hi.astype(F32)).astype(BF16)
    return hi, lo


def _dot3(q, y):
    qh, ql = _split(q)
    yh, yl = _split(y)
    return (jnp.dot(qh, yh, preferred_element_type=F32)
            + jnp.dot(qh, yl, preferred_element_type=F32)
            + jnp.dot(ql, yh, preferred_element_type=F32))


def _combine_kernel(start_ref, comp_ref, base_ref, ctile_ref, eid_ref, rank_ref, gate_ref, x_ref,
                    mod_ref, ng_ref, y_ref, o_ref, win_scr, ovf_scr, f_scr, sems, sync_sem):
    i = pl.program_id(0)
    n_steps = pl.num_programs(0)
    slot = i % 2
    w = WINDOW
    t = x_ref.shape[0]
    e0 = eid_ref[:, 0:1]
    e1 = eid_ref[:, 1:2]
    r0 = rank_ref[:, 0:1]
    r1 = rank_ref[:, 1:2]
    g0 = gate_ref[:, 0:1]
    g1 = gate_ref[:, 1:2]
    col = lax.broadcasted_iota(I32, (t, w), 1)

    def first_row(step, e, second):
        want = base_ref[step * N_EXPERTS + e] + (w if second else 0)
        return start_ref[e] + jnp.minimum(want, comp_ref[e] - w)

    def window(step_slot, e, step):
        return _window_copy(_rows_at(y_ref, first_row(step, e, False), w),
                            win_scr.at[step_slot, e], sems.at[step_slot])

    @pl.when(i == 0)
    def _():
        for e in range(N_EXPERTS):
            window(slot, e, i).start()

    @pl.when(i + 1 < n_steps)
    def _():
        for e in range(N_EXPERTS):
            window(1 - slot, e, i + 1).start()

    for e in range(N_EXPERTS):
        window(slot, e, i).wait()

    def weights(e, second):
        mine0 = e0 == e
        mine1 = e1 == e
        rank = jnp.where(mine0, r0, jnp.where(mine1, r1, -1))
        gate = jnp.where(mine0, g0, jnp.where(mine1, g1, 0.0))
        local = rank - base_ref[i * N_EXPERTS + e]
        in_window = (local >= w) if second else jnp.logical_and(local >= 0, local < w)
        pos = rank + start_ref[e] - first_row(i, e, second)
        return jnp.where(jnp.logical_and(in_window, pos == col), gate, 0.0)

    f = jnp.zeros((t, x_ref.shape[1]), F32)
    for e in range(0, N_EXPERTS, 2):
        q = jnp.concatenate([weights(e, False), weights(e + 1, False)], axis=1)
        y = jnp.concatenate([_from_tiles(win_scr.at[slot, e], w),
                             _from_tiles(win_scr.at[slot, e + 1], w)], axis=0)
        f = f + _dot3(q, y)
    f_scr[...] = f

    for e in range(N_EXPERTS):
        @pl.when(ctile_ref[i * N_EXPERTS + e] > w)
        def _():
            cp = _window_copy(_rows_at(y_ref, first_row(i, e, True), w), ovf_scr, sync_sem)
            cp.start()
            cp.wait()
            f_scr[...] += _dot3(weights(e, True), _from_tiles(ovf_scr, w))

    g2 = mod_ref[0, 5:6, :]
    o_ref[...] = x_ref[...] + g2 * _rms(f_scr[...], ng_ref[3:4, :])


def _combine(scalars, eid_c, rank_c, gate_c, x2d, seq, mod, ng, y):
    n, d = x2d.shape
    t = MOVE_ROWS
    w = WINDOW
    assert seq % t == 0 and t <= 2 * w
    per_seq = seq // t
    grid_spec = pltpu.PrefetchScalarGridSpec(
        num_scalar_prefetch=4,
        grid=(n // t,),
        in_specs=[pl.BlockSpec((t, TOP_K), lambda i, *_: (i, 0)),
                  pl.BlockSpec((t, TOP_K), lambda i, *_: (i, 0)),
                  pl.BlockSpec((t, TOP_K), lambda i, *_: (i, 0)),
                  pl.BlockSpec((t, d), lambda i, *_: (i, 0)),
                  pl.BlockSpec((1, 6, d), lambda i, *_: (i // per_seq, 0, 0)),
                  pl.BlockSpec((4, d), lambda i, *_: (0, 0)),
                  pl.BlockSpec(memory_space=pl.ANY)],
        out_specs=pl.BlockSpec((t, d), lambda i, *_: (i, 0)),
        scratch_shapes=[pltpu.VMEM((2, N_EXPERTS, w * SUBLANES, LANES), F32),
                        pltpu.VMEM((w * SUBLANES, LANES), F32),
                        pltpu.VMEM((t, d), F32),
                        pltpu.SemaphoreType.DMA((2,)),
                        pltpu.SemaphoreType.DMA],
    )
    return pl.pallas_call(
        _combine_kernel,
        grid_spec=grid_spec,
        out_shape=jax.ShapeDtypeStruct((n, d), F32),
        compiler_params=_cparams(("arbitrary",)),
        name="moe_combine",
    )(*scalars, eid_c, rank_c, gate_c, x2d, mod, ng, y)


def _moe(x2d, seq, mod, ng, router_w, router_b, wg, wu, wd):
    n, d = x2d.shape
    tm = MOE_ROWS
    h_bf, eid, gate, rank, base, cnt = _route(x2d, seq, mod, ng, router_w, router_b)
    counts = cnt[:, 0]
    comp = jnp.maximum(((counts + tm - 1) // tm) * tm, tm)
    ends = jnp.cumsum(comp + tm)
    starts = ends - (comp + tm)
    n_tiles = (TOP_K * n) // tm + 2 * N_EXPERTS
    tile_start = jnp.arange(n_tiles, dtype=I32) * tm
    tile_expert = jnp.minimum(jnp.sum(tile_start[:, None] >= ends[None, :], axis=1),
                              N_EXPERTS - 1).astype(I32)
    tile_valid = jnp.logical_and(tile_start < (starts + comp)[tile_expert],
                                 tile_start < ends[-1]).astype(I32)
    base = base[:, :, 0]
    ctile = jnp.concatenate([base[1:], counts[None, :]], axis=0) - base
    base = base.reshape(-1)
    ctile = ctile.reshape(-1)
    hs = _dispatch((starts, counts, comp, base, ctile), h_bf, eid, rank, n_tiles * tm)
    tile_block = lax.cummax(jnp.where(tile_valid > 0, jnp.arange(n_tiles, dtype=I32), 0))
    y = _experts(tile_expert, tile_valid, tile_block, hs, wg, wu, wd)
    return _combine((starts, comp, base, ctile), eid.T, rank.T, gate.T, x2d, seq, mod, ng, y)


def kernel(x, c, w_ada, b_ada, norm_gain, w_in, b_in, ln_v_gain, ln_v_bias, w_spatial, b_spatial, conv_w, conv_b, ln_conv_gain, ln_conv_bias, group_gain, w_out, ffn_w_gate, ffn_w_up, ffn_w_down, router_w, router_b, moe_w_gate, moe_w_up, moe_w_down):
    bsz, seq, d = x.shape
    depth = w_ada.shape[0]
    mod_all = _ada(c, w_ada, b_ada).reshape(depth, bsz, 6, d)
    for l in range(depth):
        mod = mod_all[l]
        ng = norm_gain[l]
        x = _mix(x, mod, ng, w_in[l], b_in[l], ln_v_gain[l], ln_v_bias[l], w_spatial[l],
                 b_spatial[l], conv_w[l], conv_b[l], ln_conv_gain[l], ln_conv_bias[l],
                 group_gain[l], w_out[l])
        x2d = x.reshape(bsz * seq, d)
        i = l // 2
        if l % 2 == 0:
            x2d = _ffn(x2d, seq, mod, ng, ffn_w_gate[i], ffn_w_up[i], ffn_w_down[i])
        else:
            x2d = _moe(x2d, seq, mod, ng, router_w[i], router_b[i], moe_w_gate[i],
                       moe_w_up[i], moe_w_down[i])
        x = x2d.reshape(bsz, seq, d)
    return x
```

```python
import jax
import jax.numpy as jnp
from jax import lax
from jax.experimental import pallas as pl
from jax.experimental.pallas import tpu as pltpu

F32 = jnp.float32
BF16 = jnp.bfloat16
I32 = jnp.int32

EPS = 1e-6
CHUNK = 64
GMLP_BLOCK = 128
N_HEADS_A = 8
CONV_WIDTH = 31
N_EXPERTS = 8
TOP_K = 2

LANES = 128
SUBLANES = 8
CONV_HALO = 32
VMEM_LIMIT = 56 * 1024 * 1024

SEQ_TILE = 512
SUB_TILE = 256
FFN_ROWS = 512
FFN_COLS = 1792
MOE_ROWS = 512
MOVE_ROWS = 256
WINDOW = 96
WINDOW_SLOT = 128


def _rms(x, g):
    return x * lax.rsqrt(jnp.mean(x * x, axis=-1, keepdims=True) + EPS) * g


def _layer_norm(x, g, b):
    mu = jnp.mean(x, axis=-1, keepdims=True)
    xc = x - mu
    return xc * lax.rsqrt(jnp.mean(xc * xc, axis=-1, keepdims=True) + EPS) * g + b


_SQRT_2_OVER_PI = 0.7978845608028654


def _sigmoid(x):
    return 0.5 + 0.5 * jnp.tanh(0.5 * x)


def _silu(x):
    return x * _sigmoid(x)


def _gelu(x):
    inner = x * (_SQRT_2_OVER_PI + (_SQRT_2_OVER_PI * 0.044715) * (x * x))
    half = 0.5 * x
    return half + half * jnp.tanh(inner)


def _cparams(sem, vmem=VMEM_LIMIT):
    return pltpu.CompilerParams(dimension_semantics=sem, vmem_limit_bytes=vmem)


def _ada_kernel(c_ref, w_ref, b_ref, o_ref):
    c_act = jax.nn.silu(c_ref[...])
    o_ref[0] = jnp.dot(c_act, w_ref[0], preferred_element_type=F32,
                       precision=lax.Precision.HIGHEST) + b_ref[0]


def _ada(c, w_ada, b_ada):
    depth, d, n6 = w_ada.shape
    bsz = c.shape[0]
    nc = 1536
    return pl.pallas_call(
        _ada_kernel,
        grid=(depth, n6 // nc),
        in_specs=[pl.BlockSpec((bsz, d), lambda l, j: (0, 0)),
                  pl.BlockSpec((1, d, nc), lambda l, j: (l, 0, j)),
                  pl.BlockSpec((1, 1, nc), lambda l, j: (l, 0, j))],
        out_specs=pl.BlockSpec((1, bsz, nc), lambda l, j: (l, 0, j)),
        out_shape=jax.ShapeDtypeStruct((depth, bsz, n6), F32),
        compiler_params=_cparams(("arbitrary", "arbitrary")),
        name="ada_mod",
    )(c, w_ada, b_ada.reshape(depth, 1, n6))


def _mix_kernel(x_ref, mod_ref, ng_ref, w_in_ref, b_in_ref, lnv_g_ref, lnv_b_ref, ws_ref,
                bs_ref, cw_ref, cb_ref, lnc_g_ref, lnc_b_ref, gg_ref, w_out_ref,
                o_ref, wsp_scr, xg_scr, sh_scr, yc_scr):
    ts = x_ref.shape[1]
    d_a = lnv_g_ref.shape[1]
    d_b = lnc_g_ref.shape[1]
    b = pl.program_id(0)
    s = pl.program_id(1)

    @pl.when(jnp.logical_and(b == 0, s == 0))
    def _():
        t_chunk = lax.broadcasted_iota(I32, (GMLP_BLOCK, GMLP_BLOCK), 0) // CHUNK
        s_chunk = lax.broadcasted_iota(I32, (GMLP_BLOCK, GMLP_BLOCK), 1) // CHUNK
        allowed = t_chunk >= s_chunk
        for j in range(N_HEADS_A // 2):
            lo = jnp.where(allowed, ws_ref[2 * j], 0.0).astype(BF16)
            hi = jnp.where(allowed, ws_ref[2 * j + 1], 0.0).astype(BF16)
            wsp_scr[j] = jnp.concatenate([lo, hi], axis=1)

    @pl.when(s == 0)
    def _():
        xg_scr[0:CONV_HALO, :] = jnp.zeros((CONV_HALO, d_b), F32)

    sh1 = mod_ref[0, 0:1, :]
    sc1 = mod_ref[0, 1:2, :]
    g1 = mod_ref[0, 2:3, :]
    head_dim = d_a // N_HEADS_A
    lane = lax.broadcasted_iota(I32, (GMLP_BLOCK, LANES), 1)
    first_head = lane < head_dim
    zero = jnp.zeros((GMLP_BLOCK, LANES), BF16)
    first_tap = CONV_HALO - (CONV_WIDTH - 1)
    keep = CONV_HALO - SUBLANES
    rows = 64
    sub = min(SUB_TILE, ts)

    z_all = []
    for q in range(ts // sub):
        x = x_ref[0, q * sub:(q + 1) * sub, :]
        h = _rms(x, ng_ref[0:1, :]) * (1.0 + sc1) + sh1
        z_all.append(jnp.dot(h.astype(BF16), w_in_ref[...], preferred_element_type=F32)
                     + b_in_ref[...])

    for q in range(ts // sub):
        lo = q * sub
        x = x_ref[0, lo:lo + sub, :]
        z = z_all[q]
        ua = z[:, 0:d_a]
        va = z[:, d_a:2 * d_a]
        ab = z[:, 2 * d_a:2 * d_a + d_b]
        gb = z[:, 2 * d_a + d_b:]

        u = _gelu(ua)
        v = _layer_norm(_gelu(va), lnv_g_ref[...], lnv_b_ref[...]).astype(BF16)
        blocks = []
        for n in range(sub // GMLP_BLOCK):
            cols = []
            for j in range(d_a // LANES):
                vc = v[n * GMLP_BLOCK:(n + 1) * GMLP_BLOCK, j * LANES:(j + 1) * LANES]
                rhs = jnp.concatenate([jnp.where(first_head, vc, zero),
                                       jnp.where(first_head, zero, vc)], axis=0)
                cols.append(jnp.dot(wsp_scr[j], rhs, preferred_element_type=F32))
            blocks.append(jnp.concatenate(cols, axis=1) + bs_ref[...])
        ya = u * jnp.concatenate(blocks, axis=0)

        xg_scr[CONV_HALO + lo:CONV_HALO + lo + sub, :] = ab * _sigmoid(gb)
        new_lo = lo if q == 0 else lo + keep
        new_hi = lo + sub + keep
        for r in range(1, SUBLANES):
            sh_scr[r, new_lo:new_hi, :] = xg_scr[new_lo + r:new_hi + r, :]
        for rc in range(sub // rows):
            base = lo + rc * rows
            for lc in range(d_b // LANES):
                ls = slice(lc * LANES, (lc + 1) * LANES)
                acc = jnp.broadcast_to(cb_ref[:, ls], (rows, LANES))
                for k in range(CONV_WIDTH):
                    off = first_tap + k
                    r = off % SUBLANES
                    r0 = base + off - r
                    if r == 0:
                        win = xg_scr[r0:r0 + rows, ls]
                    else:
                        win = sh_scr[r, r0:r0 + rows, ls]
                    acc = acc + cw_ref[k:k + 1, ls] * win
                yc_scr[base:base + rows, ls] = acc
        yb = _silu(_layer_norm(yc_scr[lo:lo + sub, :], lnc_g_ref[...], lnc_b_ref[...]))

        ycat = jnp.concatenate([_rms(ya, gg_ref[:, 0:d_a]), _rms(yb, gg_ref[:, d_a:])], axis=1)
        y = jnp.dot(ycat.astype(BF16), w_out_ref[...], preferred_element_type=F32)
        o_ref[0, lo:lo + sub, :] = x + g1 * _rms(y, ng_ref[1:2, :])

    xg_scr[0:CONV_HALO, :] = xg_scr[ts:ts + CONV_HALO, :]


def _mix(x, mod, ng, w_in, b_in, lnv_g, lnv_b, w_sp, b_sp, cw, cb, lnc_g, lnc_b, gg, w_out):
    bsz, seq, d = x.shape
    d_in = w_in.shape[1]
    d_a = lnv_g.shape[0]
    d_b = lnc_g.shape[0]
    ts = min(SEQ_TILE, seq)
    assert seq % ts == 0 and ts % GMLP_BLOCK == 0 and ts >= CONV_HALO
    bs_full = jnp.repeat(b_sp.T, d_a // N_HEADS_A, axis=1)
    const = lambda *shape: pl.BlockSpec(shape, lambda b, s: (0,) * len(shape))
    return pl.pallas_call(
        _mix_kernel,
        grid=(bsz, seq // ts),
        in_specs=[pl.BlockSpec((1, ts, d), lambda b, s: (b, s, 0)),
                  pl.BlockSpec((1, 6, d), lambda b, s: (b, 0, 0)),
                  const(4, d), const(d, d_in), const(1, d_in), const(1, d_a), const(1, d_a),
                  const(N_HEADS_A, GMLP_BLOCK, GMLP_BLOCK), const(GMLP_BLOCK, d_a),
                  const(CONV_WIDTH, d_b), const(1, d_b), const(1, d_b), const(1, d_b),
                  const(1, d_a + d_b), const(d_a + d_b, d)],
        out_specs=pl.BlockSpec((1, ts, d), lambda b, s: (b, s, 0)),
        out_shape=jax.ShapeDtypeStruct((bsz, seq, d), F32),
        scratch_shapes=[pltpu.VMEM((N_HEADS_A // 2, GMLP_BLOCK, 2 * GMLP_BLOCK), BF16),
                        pltpu.VMEM((ts + CONV_HALO, d_b), F32),
                        pltpu.VMEM((SUBLANES, ts + CONV_HALO, d_b), F32),
                        pltpu.VMEM((ts, d_b), F32)],
        compiler_params=_cparams(("arbitrary", "arbitrary")),
        name="token_mix",
    )(x, mod, ng, w_in.astype(BF16), b_in.reshape(1, d_in), lnv_g.reshape(1, d_a),
      lnv_b.reshape(1, d_a), w_sp, bs_full, cw, cb.reshape(1, d_b), lnc_g.reshape(1, d_b),
      lnc_b.reshape(1, d_b), gg.reshape(1, d_a + d_b), w_out.astype(BF16))


def _swiglu_chunk(h, wg, wu, wd):
    g = jnp.dot(h, wg, preferred_element_type=F32)
    u = jnp.dot(h, wu, preferred_element_type=F32)
    a = (_silu(g) * u).astype(BF16)
    return jnp.dot(a, wd, preferred_element_type=F32)


def _ffn_kernel(x_ref, mod_ref, ng_ref, wg_ref, wu_ref, wd_ref, o_ref, h_scr, acc_scr):
    j = pl.program_id(1)

    @pl.when(j == 0)
    def _():
        sh2 = mod_ref[0, 3:4, :]
        sc2 = mod_ref[0, 4:5, :]
        h = _rms(x_ref[...], ng_ref[2:3, :]) * (1.0 + sc2) + sh2
        h_scr[...] = h.astype(BF16)
        acc_scr[...] = jnp.zeros_like(acc_scr)

    acc_scr[...] += _swiglu_chunk(h_scr[...], wg_ref[...], wu_ref[...], wd_ref[...])

    @pl.when(j == pl.num_programs(1) - 1)
    def _():
        g2 = mod_ref[0, 5:6, :]
        o_ref[...] = x_ref[...] + g2 * _rms(acc_scr[...], ng_ref[3:4, :])


def _ffn(x2d, seq, mod, ng, wg, wu, wd):
    n, d = x2d.shape
    f = wg.shape[1]
    tm = min(FFN_ROWS, seq)
    fc = FFN_COLS
    assert seq % tm == 0 and f % fc == 0
    per_seq = seq // tm
    return pl.pallas_call(
        _ffn_kernel,
        grid=(n // tm, f // fc),
        in_specs=[pl.BlockSpec((tm, d), lambda i, j: (i, 0)),
                  pl.BlockSpec((1, 6, d), lambda i, j: (i // per_seq, 0, 0)),
                  pl.BlockSpec((4, d), lambda i, j: (0, 0)),
                  pl.BlockSpec((d, fc), lambda i, j: (0, j)),
                  pl.BlockSpec((d, fc), lambda i, j: (0, j)),
                  pl.BlockSpec((fc, d), lambda i, j: (j, 0))],
        out_specs=pl.BlockSpec((tm, d), lambda i, j: (i, 0)),
        out_shape=jax.ShapeDtypeStruct((n, d), F32),
        scratch_shapes=[pltpu.VMEM((tm, d), BF16), pltpu.VMEM((tm, d), F32)],
        compiler_params=_cparams(("arbitrary", "arbitrary")),
        name="ffn_dense",
    )(x2d, mod, ng, wg.astype(BF16), wu.astype(BF16), wd.astype(BF16))


def _route_kernel(x_ref, mod_ref, ng_ref, rw_ref, rb_ref, h_ref, eid_ref, gate_ref, rank_ref,
                  base_ref, cnt_ref, run_scr):
    i = pl.program_id(0)
    tr = x_ref.shape[0]

    @pl.when(i == 0)
    def _():
        run_scr[...] = jnp.zeros_like(run_scr)

    base_ref[0] = run_scr[...].astype(I32)
    sh2 = mod_ref[0, 3:4, :]
    sc2 = mod_ref[0, 4:5, :]
    h = _rms(x_ref[...], ng_ref[2:3, :]) * (1.0 + sc2) + sh2
    h_ref[...] = h.astype(BF16)
    logits = lax.dot_general(rw_ref[...], h, (((1,), (1,)), ((), ())),
                             preferred_element_type=F32,
                             precision=lax.Precision.HIGHEST) + rb_ref[...]
    e_iota = lax.broadcasted_iota(I32, logits.shape, 0)
    m1 = jnp.max(logits, axis=0, keepdims=True)
    i1 = jnp.min(jnp.where(logits == m1, e_iota, N_EXPERTS), axis=0, keepdims=True)
    oh1 = e_iota == i1
    rest = jnp.where(oh1, -jnp.inf, logits)
    m2 = jnp.max(rest, axis=0, keepdims=True)
    i2 = jnp.min(jnp.where(rest == m2, e_iota, N_EXPERTS), axis=0, keepdims=True)
    oh2 = e_iota == i2
    e2 = jnp.exp(m2 - m1)
    den = 1.0 + e2
    gate_ref[...] = jnp.concatenate([1.0 / den, e2 / den], axis=0)
    eid_ref[...] = jnp.concatenate([i1, i2], axis=0)

    member = oh1.astype(F32) + oh2.astype(F32)
    before = (lax.broadcasted_iota(I32, (tr, tr), 0) <
              lax.broadcasted_iota(I32, (tr, tr), 1)).astype(BF16)
    prefix = jnp.dot(member.astype(BF16), before, preferred_element_type=F32) + run_scr[:, 0:1]
    r1 = jnp.sum(jnp.where(oh1, prefix, 0.0), axis=0, keepdims=True)
    r2 = jnp.sum(jnp.where(oh2, prefix, 0.0), axis=0, keepdims=True)
    rank_ref[...] = jnp.concatenate([r1, r2], axis=0).astype(I32)
    run_scr[...] += jnp.sum(member, axis=1, keepdims=True)
    cnt_ref[...] = run_scr[...].astype(I32)


def _route(x2d, seq, mod, ng, router_w, router_b):
    n, d = x2d.shape
    tr = MOVE_ROWS
    assert seq % tr == 0
    per_seq = seq // tr
    return pl.pallas_call(
        _route_kernel,
        grid=(n // tr,),
        in_specs=[pl.BlockSpec((tr, d), lambda i: (i, 0)),
                  pl.BlockSpec((1, 6, d), lambda i: (i // per_seq, 0, 0)),
                  pl.BlockSpec((4, d), lambda i: (0, 0)),
                  pl.BlockSpec((N_EXPERTS, d), lambda i: (0, 0)),
                  pl.BlockSpec((N_EXPERTS, 1), lambda i: (0, 0))],
        out_specs=[pl.BlockSpec((tr, d), lambda i: (i, 0)),
                   pl.BlockSpec((TOP_K, tr), lambda i: (0, i)),
                   pl.BlockSpec((TOP_K, tr), lambda i: (0, i)),
                   pl.BlockSpec((TOP_K, tr), lambda i: (0, i)),
                   pl.BlockSpec((1, N_EXPERTS, LANES), lambda i: (i, 0, 0)),
                   pl.BlockSpec((N_EXPERTS, LANES), lambda i: (0, 0))],
        out_shape=[jax.ShapeDtypeStruct((n, d), BF16),
                   jax.ShapeDtypeStruct((TOP_K, n), I32),
                   jax.ShapeDtypeStruct((TOP_K, n), F32),
                   jax.ShapeDtypeStruct((TOP_K, n), I32),
                   jax.ShapeDtypeStruct((n // tr, N_EXPERTS, LANES), I32),
                   jax.ShapeDtypeStruct((N_EXPERTS, LANES), I32)],
        scratch_shapes=[pltpu.VMEM((N_EXPERTS, LANES), F32)],
        compiler_params=_cparams(("arbitrary",)),
        name="moe_route",
    )(x2d, mod, ng, router_w.T, router_b.reshape(N_EXPERTS, 1))


def _window_copy(src, dst, sem):
    return pltpu.make_async_copy(src, dst, sem)


def _rows_at(ref, row, n_rows):
    return ref.at[pl.ds(pl.multiple_of(row * SUBLANES, SUBLANES), n_rows * SUBLANES), :]


def _to_tiles(dst, val):
    rows = val.shape[0]
    for jj in range(val.shape[1] // LANES):
        dst[pl.ds(jj, rows, stride=SUBLANES), :] = val[:, jj * LANES:(jj + 1) * LANES]


def _from_tiles(src, rows):
    return jnp.concatenate([src[pl.ds(jj, rows, stride=SUBLANES), :] for jj in range(SUBLANES)],
                           axis=1)


def _dispatch_kernel(start_ref, count_ref, comp_ref, base_ref, ctile_ref, h_ref, eid_ref,
                     rank_ref, hs_ref, win_scr, ovf_scr, zero_scr, sems, sync_sem):
    i = pl.program_id(0)
    n_steps = pl.num_programs(0)
    slot = i % 2
    w = WINDOW
    t = h_ref.shape[0]
    h = h_ref[...]
    e0 = eid_ref[0:1, :]
    e1 = eid_ref[1:2, :]
    r0 = rank_ref[0:1, :]
    r1 = rank_ref[1:2, :]
    row = lax.broadcasted_iota(I32, (w, t), 0)

    def local_rank(e):
        return jnp.where(e0 == e, r0, jnp.where(e1 == e, r1, -1)) - base_ref[i * N_EXPERTS + e]

    def selector(e, first_row):
        return jnp.where(row + first_row == local_rank(e), 1.0, 0.0).astype(BF16)

    slab = jnp.dot(jnp.concatenate([selector(e, 0) for e in range(N_EXPERTS)], axis=0), h,
                   preferred_element_type=F32)
    for e in range(N_EXPERTS):
        _to_tiles(win_scr.at[slot, e], slab[e * w:(e + 1) * w, :])

    def window(step_slot, e, step):
        dst_row = start_ref[e] + base_ref[step * N_EXPERTS + e]
        return _window_copy(win_scr.at[step_slot, e], _rows_at(hs_ref, dst_row, w),
                            sems.at[step_slot])

    @pl.when(i > 0)
    def _():
        for e in range(N_EXPERTS):
            window(1 - slot, e, i - 1).wait()

    for e in range(N_EXPERTS):
        window(slot, e, i).start()

    for e in range(N_EXPERTS):
        run = ctile_ref[i * N_EXPERTS + e]

        @pl.when(run > w)
        def _():
            def extra(k, c):
                first = k * w
                _to_tiles(ovf_scr, jnp.dot(selector(e, first), h, preferred_element_type=F32))
                dst_row = start_ref[e] + base_ref[i * N_EXPERTS + e] + first
                cp = _window_copy(ovf_scr, _rows_at(hs_ref, dst_row, w), sync_sem)
                cp.start()
                cp.wait()
                return c

            lax.fori_loop(1, (run + w - 1) // w, extra, 0)

    @pl.when(i == n_steps - 1)
    def _():
        for e in range(N_EXPERTS):
            window(slot, e, i).wait()
        zero_scr[...] = jnp.zeros_like(zero_scr)
        tm = zero_scr.shape[0] // SUBLANES

        def zero_fill(row_start):
            cp = _window_copy(zero_scr, _rows_at(hs_ref, row_start, tm), sync_sem)
            cp.start()
            cp.wait()

        for e in range(N_EXPERTS):
            zero_fill(start_ref[e] + count_ref[e])
            zero_fill(start_ref[e] + comp_ref[e])
        used = start_ref[N_EXPERTS - 1] + comp_ref[N_EXPERTS - 1] + tm
        total = hs_ref.shape[0] // SUBLANES

        def tail(k, c):
            zero_fill(used + k * tm)
            return c

        lax.fori_loop(0, (total - used) // tm, tail, 0)


def _dispatch(scalars, h_bf, eid, rank, n_rows):
    n, d = h_bf.shape
    t = MOVE_ROWS
    w = WINDOW
    assert n % t == 0 and d == SUBLANES * LANES and w % (2 * SUBLANES) == 0 and w <= MOE_ROWS
    grid_spec = pltpu.PrefetchScalarGridSpec(
        num_scalar_prefetch=5,
        grid=(n // t,),
        in_specs=[pl.BlockSpec((t, d), lambda i, *_: (i, 0)),
                  pl.BlockSpec((TOP_K, t), lambda i, *_: (0, i)),
                  pl.BlockSpec((TOP_K, t), lambda i, *_: (0, i))],
        out_specs=pl.BlockSpec(memory_space=pl.ANY),
        scratch_shapes=[pltpu.VMEM((2, N_EXPERTS, w * SUBLANES, LANES), F32),
                        pltpu.VMEM((w * SUBLANES, LANES), F32),
                        pltpu.VMEM((MOE_ROWS * SUBLANES, LANES), F32),
                        pltpu.SemaphoreType.DMA((2,)),
                        pltpu.SemaphoreType.DMA],
    )
    return pl.pallas_call(
        _dispatch_kernel,
        grid_spec=grid_spec,
        out_shape=jax.ShapeDtypeStruct((n_rows * SUBLANES, LANES), F32),
        compiler_params=_cparams(("arbitrary",)),
        name="moe_dispatch",
    )(*scalars, h_bf, eid, rank)


def _expert_kernel(te_ref, tv_ref, tb_ref, hs_ref, wg_ref, wu_ref, wd_ref, *rest):
    del te_ref, tb_ref
    y_ref = rest[-1]
    i = pl.program_id(0)
    tm = hs_ref.shape[0] // SUBLANES

    @pl.when(tv_ref[i] > 0)
    def _():
        h = _from_tiles(hs_ref, tm).astype(BF16)
        part = _swiglu_chunk(h, wg_ref[0], wu_ref[0], wd_ref[0])
        if len(rest) == 2:
            part = part + _from_tiles(rest[0], tm)
        _to_tiles(y_ref, part)

    @pl.when(tv_ref[i] == 0)
    def _():
        y_ref[...] = jnp.zeros_like(y_ref)


def _experts(tile_expert, tile_valid, tile_block, hs, wg, wu, wd):
    d = wg.shape[1]
    f = wg.shape[2]
    tm = MOE_ROWS
    fc = FFN_COLS
    n_tiles = hs.shape[0] // (tm * SUBLANES)
    wg, wu, wd = wg.astype(BF16), wu.astype(BF16), wd.astype(BF16)
    tile_in = pl.BlockSpec((tm * SUBLANES, LANES), lambda i, te, tv, tb: (tb[i], 0))
    y = None
    for c in range(f // fc):
        in_specs = [tile_in,
                    pl.BlockSpec((1, d, fc), lambda i, te, tv, tb, c=c: (te[i], 0, c)),
                    pl.BlockSpec((1, d, fc), lambda i, te, tv, tb, c=c: (te[i], 0, c)),
                    pl.BlockSpec((1, fc, d), lambda i, te, tv, tb, c=c: (te[i], c, 0))]
        operands = [hs, wg, wu, wd]
        if y is not None:
            in_specs.append(tile_in)
            operands.append(y)
        grid_spec = pltpu.PrefetchScalarGridSpec(
            num_scalar_prefetch=3,
            grid=(n_tiles,),
            in_specs=in_specs,
            out_specs=pl.BlockSpec((tm * SUBLANES, LANES), lambda i, te, tv, tb: (i, 0)),
        )
        y = pl.pallas_call(
            _expert_kernel,
            grid_spec=grid_spec,
            out_shape=jax.ShapeDtypeStruct(hs.shape, F32),
            compiler_params=_cparams(("arbitrary",)),
            name="moe_experts",
        )(tile_expert, tile_valid, tile_block, *operands)
    return y


def _split(v):
    hi = v.astype(BF16)
    lo = (v - hi.astype(F32)).astype(BF16)
    return hi, lo


def _dot3(q, y):
    qh, ql = _split(q)
    yh, yl = _split(y)
    return (jnp.dot(qh, yh, preferred_element_type=F32)
            + jnp.dot(qh, yl, preferred_element_type=F32)
            + jnp.dot(ql, yh, preferred_element_type=F32))


def _combine_kernel(start_ref, comp_ref, base_ref, ctile_ref, eid_ref, rank_ref, gate_ref, x_ref,
                    mod_ref, ng_ref, y_ref, o_ref, win_scr, ovf_scr, f_scr, sems, sync_sem):
    i = pl.program_id(0)
    n_steps = pl.num_programs(0)
    slot = i % 2
    w = WINDOW
    t = x_ref.shape[0]
    e0 = eid_ref[:, 0:1]
    e1 = eid_ref[:, 1:2]
    r0 = rank_ref[:, 0:1]
    r1 = rank_ref[:, 1:2]
    g0 = gate_ref[:, 0:1]
    g1 = gate_ref[:, 1:2]
    col = lax.broadcasted_iota(I32, (t, WINDOW_SLOT), 1)

    def first_row(step, e, k):
        want = base_ref[step * N_EXPERTS + e] + k * w
        return start_ref[e] + jnp.minimum(want, comp_ref[e] - w)

    def window(step_slot, e, step):
        dst = win_scr.at[step_slot, pl.ds(e * WINDOW_SLOT * SUBLANES, w * SUBLANES), :]
        return _window_copy(_rows_at(y_ref, first_row(step, e, 0), w), dst, sems.at[step_slot])

    @pl.when(i == 0)
    def _():
        win_scr[...] = jnp.zeros_like(win_scr)
        ovf_scr[...] = jnp.zeros_like(ovf_scr)
        for e in range(N_EXPERTS):
            window(slot, e, i).start()

    @pl.when(i + 1 < n_steps)
    def _():
        for e in range(N_EXPERTS):
            window(1 - slot, e, i + 1).start()

    for e in range(N_EXPERTS):
        window(slot, e, i).wait()

    def weights(e, k):
        mine0 = e0 == e
        mine1 = e1 == e
        rank = jnp.where(mine0, r0, jnp.where(mine1, r1, -1))
        gate = jnp.where(mine0, g0, jnp.where(mine1, g1, 0.0))
        local = rank - base_ref[i * N_EXPERTS + e]
        in_window = jnp.logical_and(local >= k * w, local < (k + 1) * w)
        pos = rank + start_ref[e] - first_row(i, e, k)
        return jnp.where(jnp.logical_and(in_window, pos == col), gate, 0.0)

    q_all = jnp.concatenate([weights(e, 0) for e in range(N_EXPERTS)], axis=1)
    f_scr[...] = _dot3(q_all, _from_tiles(win_scr.at[slot], N_EXPERTS * WINDOW_SLOT))

    for e in range(N_EXPERTS):
        run = ctile_ref[i * N_EXPERTS + e]

        @pl.when(run > w)
        def _():
            def extra(k, c):
                cp = _window_copy(_rows_at(y_ref, first_row(i, e, k), w),
                                  ovf_scr.at[pl.ds(0, w * SUBLANES), :], sync_sem)
                cp.start()
                cp.wait()
                f_scr[...] += _dot3(weights(e, k), _from_tiles(ovf_scr, WINDOW_SLOT))
                return c

            lax.fori_loop(1, (run + w - 1) // w, extra, 0)

    g2 = mod_ref[0, 5:6, :]
    o_ref[...] = x_ref[...] + g2 * _rms(f_scr[...], ng_ref[3:4, :])


def _combine(scalars, eid_c, rank_c, gate_c, x2d, seq, mod, ng, y):
    n, d = x2d.shape
    t = MOVE_ROWS
    w = WINDOW
    assert seq % t == 0 and w <= WINDOW_SLOT and w <= MOE_ROWS
    per_seq = seq // t
    grid_spec = pltpu.PrefetchScalarGridSpec(
        num_scalar_prefetch=4,
        grid=(n // t,),
        in_specs=[pl.BlockSpec((t, TOP_K), lambda i, *_: (i, 0)),
                  pl.BlockSpec((t, TOP_K), lambda i, *_: (i, 0)),
                  pl.BlockSpec((t, TOP_K), lambda i, *_: (i, 0)),
                  pl.BlockSpec((t, d), lambda i, *_: (i, 0)),
                  pl.BlockSpec((1, 6, d), lambda i, *_: (i // per_seq, 0, 0)),
                  pl.BlockSpec((4, d), lambda i, *_: (0, 0)),
                  pl.BlockSpec(memory_space=pl.ANY)],
        out_specs=pl.BlockSpec((t, d), lambda i, *_: (i, 0)),
        scratch_shapes=[pltpu.VMEM((2, N_EXPERTS * WINDOW_SLOT * SUBLANES, LANES), F32),
                        pltpu.VMEM((WINDOW_SLOT * SUBLANES, LANES), F32),
                        pltpu.VMEM((t, d), F32),
                        pltpu.SemaphoreType.DMA((2,)),
                        pltpu.SemaphoreType.DMA],
    )
    return pl.pallas_call(
        _combine_kernel,
        grid_spec=grid_spec,
        out_shape=jax.ShapeDtypeStruct((n, d), F32),
        compiler_params=_cparams(("arbitrary",)),
        name="moe_combine",
    )(*scalars, eid_c, rank_c, gate_c, x2d, mod, ng, y)


def _moe(x2d, seq, mod, ng, router_w, router_b, wg, wu, wd):
    n, d = x2d.shape
    tm = MOE_ROWS
    h_bf, eid, gate, rank, base, cnt = _route(x2d, seq, mod, ng, router_w, router_b)
    counts = cnt[:, 0]
    comp = jnp.maximum(((counts + tm - 1) // tm) * tm, tm)
    ends = jnp.cumsum(comp + tm)
    starts = ends - (comp + tm)
    n_tiles = (TOP_K * n) // tm + 2 * N_EXPERTS
    tile_start = jnp.arange(n_tiles, dtype=I32) * tm
    tile_expert = jnp.minimum(jnp.sum(tile_start[:, None] >= ends[None, :], axis=1),
                              N_EXPERTS - 1).astype(I32)
    tile_valid = jnp.logical_and(tile_start < (starts + comp)[tile_expert],
                                 tile_start < ends[-1]).astype(I32)
    base = base[:, :, 0]
    ctile = jnp.concatenate([base[1:], counts[None, :]], axis=0) - base
    base = base.reshape(-1)
    ctile = ctile.reshape(-1)
    hs = _dispatch((starts, counts, comp, base, ctile), h_bf, eid, rank, n_tiles * tm)
    tile_block = lax.cummax(jnp.where(tile_valid > 0, jnp.arange(n_tiles, dtype=I32), 0))
    y = _experts(tile_expert, tile_valid, tile_block, hs, wg, wu, wd)
    return _combine((starts, comp, base, ctile), eid.T, rank.T, gate.T, x2d, seq, mod, ng, y)


def kernel(x, c, w_ada, b_ada, norm_gain, w_in, b_in, ln_v_gain, ln_v_bias, w_spatial, b_spatial, conv_w, conv_b, ln_conv_gain, ln_conv_bias, group_gain, w_out, ffn_w_gate, ffn_w_up, ffn_w_down, router_w, router_b, moe_w_gate, moe_w_up, moe_w_down):
    bsz, seq, d = x.shape
    depth = w_ada.shape[0]
    mod_all = _ada(c, w_ada, b_ada).reshape(depth, bsz, 6, d)
    for l in range(depth):
        mod = mod_all[l]
        ng = norm_gain[l]
        x = _mix(x, mod, ng, w_in[l], b_in[l], ln_v_gain[l], ln_v_bias[l], w_spatial[l],
                 b_spatial[l], conv_w[l], conv_b[l], ln_conv_gain[l], ln_conv_bias[l],
                 group_gain[l], w_out[l])
        x2d = x.reshape(bsz * seq, d)
        i = l // 2
        if l % 2 == 0:
            x2d = _ffn(x2d, seq, mod, ng, ffn_w_gate[i], ffn_w_up[i], ffn_w_down[i])
        else:
            x2d = _moe(x2d, seq, mod, ng, router_w[i], router_b[i], moe_w_gate[i],
                       moe_w_up[i], moe_w_down[i])
        x = x2d.reshape(bsz, seq, d)
    return x
```

```python
import jax
import jax.numpy as jnp
from jax import lax
from jax.experimental import pallas as pl
from jax.experimental.pallas import tpu as pltpu

F32 = jnp.float32
BF16 = jnp.bfloat16
I32 = jnp.int32

EPS = 1e-6
CHUNK = 64
GMLP_BLOCK = 128
N_HEADS_A = 8
CONV_WIDTH = 31
N_EXPERTS = 8
TOP_K = 2

LANES = 128
SUBLANES = 8
CONV_HALO = 32
VMEM_LIMIT = 56 * 1024 * 1024

SEQ_TILE = 512
SUB_TILE = 256
FFN_ROWS = 512
FFN_COLS = 1792
MOE_ROWS = 512
MOVE_ROWS = 256
COMBINE_WINDOW = MOVE_ROWS // 2
DISPATCH_STEPS = 2
DISPATCH_WINDOW = 192


def _rms(x, g):
    return x * lax.rsqrt(jnp.mean(x * x, axis=-1, keepdims=True) + EPS) * g


def _layer_norm(x, g, b):
    mu = jnp.mean(x, axis=-1, keepdims=True)
    xc = x - mu
    return xc * lax.rsqrt(jnp.mean(xc * xc, axis=-1, keepdims=True) + EPS) * g + b


_SQRT_2_OVER_PI = 0.7978845608028654


def _sigmoid(x):
    return 0.5 + 0.5 * jnp.tanh(0.5 * x)


def _silu(x):
    return x * _sigmoid(x)


def _gelu(x):
    inner = x * (_SQRT_2_OVER_PI + (_SQRT_2_OVER_PI * 0.044715) * (x * x))
    half = 0.5 * x
    return half + half * jnp.tanh(inner)


def _cparams(sem, vmem=VMEM_LIMIT):
    return pltpu.CompilerParams(dimension_semantics=sem, vmem_limit_bytes=vmem)


def _ada_kernel(c_ref, w_ref, b_ref, o_ref):
    c_act = jax.nn.silu(c_ref[...])
    o_ref[0] = jnp.dot(c_act, w_ref[0], preferred_element_type=F32,
                       precision=lax.Precision.HIGHEST) + b_ref[0]


def _ada(c, w_ada, b_ada):
    depth, d, n6 = w_ada.shape
    bsz = c.shape[0]
    nc = 1536
    return pl.pallas_call(
        _ada_kernel,
        grid=(depth, n6 // nc),
        in_specs=[pl.BlockSpec((bsz, d), lambda l, j: (0, 0)),
                  pl.BlockSpec((1, d, nc), lambda l, j: (l, 0, j)),
                  pl.BlockSpec((1, 1, nc), lambda l, j: (l, 0, j))],
        out_specs=pl.BlockSpec((1, bsz, nc), lambda l, j: (l, 0, j)),
        out_shape=jax.ShapeDtypeStruct((depth, bsz, n6), F32),
        compiler_params=_cparams(("arbitrary", "arbitrary")),
        name="ada_mod",
    )(c, w_ada, b_ada.reshape(depth, 1, n6))


def _mix_kernel(x_ref, mod_ref, ng_ref, w_in_ref, b_in_ref, lnv_g_ref, lnv_b_ref, ws_ref,
                bs_ref, cw_ref, cb_ref, lnc_g_ref, lnc_b_ref, gg_ref, w_out_ref,
                o_ref, wsp_scr, xg_scr, sh_scr, yc_scr):
    ts = x_ref.shape[1]
    d_a = lnv_g_ref.shape[1]
    d_b = lnc_g_ref.shape[1]
    b = pl.program_id(0)
    s = pl.program_id(1)

    @pl.when(jnp.logical_and(b == 0, s == 0))
    def _():
        t_chunk = lax.broadcasted_iota(I32, (GMLP_BLOCK, GMLP_BLOCK), 0) // CHUNK
        s_chunk = lax.broadcasted_iota(I32, (GMLP_BLOCK, GMLP_BLOCK), 1) // CHUNK
        allowed = t_chunk >= s_chunk
        for j in range(N_HEADS_A // 2):
            lo = jnp.where(allowed, ws_ref[2 * j], 0.0).astype(BF16)
            hi = jnp.where(allowed, ws_ref[2 * j + 1], 0.0).astype(BF16)
            wsp_scr[j] = jnp.concatenate([lo, hi], axis=1)

    @pl.when(s == 0)
    def _():
        xg_scr[0:CONV_HALO, :] = jnp.zeros((CONV_HALO, d_b), F32)

    sh1 = mod_ref[0, 0:1, :]
    sc1 = mod_ref[0, 1:2, :]
    g1 = mod_ref[0, 2:3, :]
    head_dim = d_a // N_HEADS_A
    lane = lax.broadcasted_iota(I32, (GMLP_BLOCK, LANES), 1)
    first_head = lane < head_dim
    zero = jnp.zeros((GMLP_BLOCK, LANES), BF16)
    first_tap = CONV_HALO - (CONV_WIDTH - 1)
    keep = CONV_HALO - SUBLANES
    rows = 64
    sub = min(SUB_TILE, ts)

    z_all = []
    for q in range(ts // sub):
        x = x_ref[0, q * sub:(q + 1) * sub, :]
        h = _rms(x, ng_ref[0:1, :]) * (1.0 + sc1) + sh1
        z_all.append(jnp.dot(h.astype(BF16), w_in_ref[...], preferred_element_type=F32)
                     + b_in_ref[...])

    for q in range(ts // sub):
        lo = q * sub
        x = x_ref[0, lo:lo + sub, :]
        z = z_all[q]
        ua = z[:, 0:d_a]
        va = z[:, d_a:2 * d_a]
        ab = z[:, 2 * d_a:2 * d_a + d_b]
        gb = z[:, 2 * d_a + d_b:]

        u = _gelu(ua)
        v = _layer_norm(_gelu(va), lnv_g_ref[...], lnv_b_ref[...]).astype(BF16)
        blocks = []
        for n in range(sub // GMLP_BLOCK):
            cols = []
            for j in range(d_a // LANES):
                vc = v[n * GMLP_BLOCK:(n + 1) * GMLP_BLOCK, j * LANES:(j + 1) * LANES]
                rhs = jnp.concatenate([jnp.where(first_head, vc, zero),
                                       jnp.where(first_head, zero, vc)], axis=0)
                cols.append(jnp.dot(wsp_scr[j], rhs, preferred_element_type=F32))
            blocks.append(jnp.concatenate(cols, axis=1) + bs_ref[...])
        ya = u * jnp.concatenate(blocks, axis=0)

        xg_scr[CONV_HALO + lo:CONV_HALO + lo + sub, :] = ab * _sigmoid(gb)
        new_lo = lo if q == 0 else lo + keep
        new_hi = lo + sub + keep
        for r in range(1, SUBLANES):
            sh_scr[r, new_lo:new_hi, :] = xg_scr[new_lo + r:new_hi + r, :]
        for rc in range(sub // rows):
            base = lo + rc * rows
            for lc in range(d_b // LANES):
                ls = slice(lc * LANES, (lc + 1) * LANES)
                acc = jnp.broadcast_to(cb_ref[:, ls], (rows, LANES))
                for k in range(CONV_WIDTH):
                    off = first_tap + k
                    r = off % SUBLANES
                    r0 = base + off - r
                    if r == 0:
                        win = xg_scr[r0:r0 + rows, ls]
                    else:
                        win = sh_scr[r, r0:r0 + rows, ls]
                    acc = acc + cw_ref[k:k + 1, ls] * win
                yc_scr[base:base + rows, ls] = acc
        yb = _silu(_layer_norm(yc_scr[lo:lo + sub, :], lnc_g_ref[...], lnc_b_ref[...]))

        ycat = jnp.concatenate([_rms(ya, gg_ref[:, 0:d_a]), _rms(yb, gg_ref[:, d_a:])], axis=1)
        y = jnp.dot(ycat.astype(BF16), w_out_ref[...], preferred_element_type=F32)
        o_ref[0, lo:lo + sub, :] = x + g1 * _rms(y, ng_ref[1:2, :])

    xg_scr[0:CONV_HALO, :] = xg_scr[ts:ts + CONV_HALO, :]


def _mix(x, mod, ng, w_in, b_in, lnv_g, lnv_b, w_sp, b_sp, cw, cb, lnc_g, lnc_b, gg, w_out):
    bsz, seq, d = x.shape
    d_in = w_in.shape[1]
    d_a = lnv_g.shape[0]
    d_b = lnc_g.shape[0]
    ts = min(SEQ_TILE, seq)
    assert seq % ts == 0 and ts % GMLP_BLOCK == 0 and ts >= CONV_HALO
    bs_full = jnp.repeat(b_sp.T, d_a // N_HEADS_A, axis=1)
    const = lambda *shape: pl.BlockSpec(shape, lambda b, s: (0,) * len(shape))
    assert ts % min(SUB_TILE, ts) == 0
    return pl.pallas_call(
        _mix_kernel,
        grid=(bsz, seq // ts),
        in_specs=[pl.BlockSpec((1, ts, d), lambda b, s: (b, s, 0)),
                  pl.BlockSpec((1, 6, d), lambda b, s: (b, 0, 0)),
                  const(4, d), const(d, d_in), const(1, d_in), const(1, d_a), const(1, d_a),
                  const(N_HEADS_A, GMLP_BLOCK, GMLP_BLOCK), const(GMLP_BLOCK, d_a),
                  const(CONV_WIDTH, d_b), const(1, d_b), const(1, d_b), const(1, d_b),
                  const(1, d_a + d_b), const(d_a + d_b, d)],
        out_specs=pl.BlockSpec((1, ts, d), lambda b, s: (b, s, 0)),
        out_shape=jax.ShapeDtypeStruct((bsz, seq, d), F32),
        scratch_shapes=[pltpu.VMEM((N_HEADS_A // 2, GMLP_BLOCK, 2 * GMLP_BLOCK), BF16),
                        pltpu.VMEM((ts + CONV_HALO, d_b), F32),
                        pltpu.VMEM((SUBLANES, ts + CONV_HALO, d_b), F32),
                        pltpu.VMEM((ts, d_b), F32)],
        compiler_params=_cparams(("arbitrary", "arbitrary")),
        name="token_mix",
    )(x, mod, ng, w_in.astype(BF16), b_in.reshape(1, d_in), lnv_g.reshape(1, d_a),
      lnv_b.reshape(1, d_a), w_sp, bs_full, cw, cb.reshape(1, d_b), lnc_g.reshape(1, d_b),
      lnc_b.reshape(1, d_b), gg.reshape(1, d_a + d_b), w_out.astype(BF16))


def _swiglu_chunk(h, wg, wu, wd):
    g = jnp.dot(h, wg, preferred_element_type=F32)
    u = jnp.dot(h, wu, preferred_element_type=F32)
    a = (_silu(g) * u).astype(BF16)
    return jnp.dot(a, wd, preferred_element_type=F32)


def _ffn_kernel(x_ref, mod_ref, ng_ref, wg_ref, wu_ref, wd_ref, o_ref, h_scr, acc_scr):
    j = pl.program_id(1)

    @pl.when(j == 0)
    def _():
        sh2 = mod_ref[0, 3:4, :]
        sc2 = mod_ref[0, 4:5, :]
        h = _rms(x_ref[...], ng_ref[2:3, :]) * (1.0 + sc2) + sh2
        h_scr[...] = h.astype(BF16)
        acc_scr[...] = jnp.zeros_like(acc_scr)

    acc_scr[...] += _swiglu_chunk(h_scr[...], wg_ref[...], wu_ref[...], wd_ref[...])

    @pl.when(j == pl.num_programs(1) - 1)
    def _():
        g2 = mod_ref[0, 5:6, :]
        o_ref[...] = x_ref[...] + g2 * _rms(acc_scr[...], ng_ref[3:4, :])


def _ffn(x2d, seq, mod, ng, wg, wu, wd):
    n, d = x2d.shape
    f = wg.shape[1]
    tm = min(FFN_ROWS, seq)
    fc = FFN_COLS
    assert seq % tm == 0 and f % fc == 0
    per_seq = seq // tm
    return pl.pallas_call(
        _ffn_kernel,
        grid=(n // tm, f // fc),
        in_specs=[pl.BlockSpec((tm, d), lambda i, j: (i, 0)),
                  pl.BlockSpec((1, 6, d), lambda i, j: (i // per_seq, 0, 0)),
                  pl.BlockSpec((4, d), lambda i, j: (0, 0)),
                  pl.BlockSpec((d, fc), lambda i, j: (0, j)),
                  pl.BlockSpec((d, fc), lambda i, j: (0, j)),
                  pl.BlockSpec((fc, d), lambda i, j: (j, 0))],
        out_specs=pl.BlockSpec((tm, d), lambda i, j: (i, 0)),
        out_shape=jax.ShapeDtypeStruct((n, d), F32),
        scratch_shapes=[pltpu.VMEM((tm, d), BF16), pltpu.VMEM((tm, d), F32)],
        compiler_params=_cparams(("arbitrary", "arbitrary")),
        name="ffn_dense",
    )(x2d, mod, ng, wg.astype(BF16), wu.astype(BF16), wd.astype(BF16))


def _route_kernel(x_ref, mod_ref, ng_ref, rw_ref, rb_ref, h_ref, eid_ref, gate_ref, rank_ref,
                  base_ref, cnt_ref, run_scr):
    i = pl.program_id(0)
    tr = x_ref.shape[0]

    @pl.when(i == 0)
    def _():
        run_scr[...] = jnp.zeros_like(run_scr)

    base_ref[0] = run_scr[...].astype(I32)
    sh2 = mod_ref[0, 3:4, :]
    sc2 = mod_ref[0, 4:5, :]
    h = _rms(x_ref[...], ng_ref[2:3, :]) * (1.0 + sc2) + sh2
    h_ref[...] = h.astype(BF16)
    logits = lax.dot_general(rw_ref[...], h, (((1,), (1,)), ((), ())),
                             preferred_element_type=F32,
                             precision=lax.Precision.HIGHEST) + rb_ref[...]
    e_iota = lax.broadcasted_iota(I32, logits.shape, 0)
    m1 = jnp.max(logits, axis=0, keepdims=True)
    i1 = jnp.min(jnp.where(logits == m1, e_iota, N_EXPERTS), axis=0, keepdims=True)
    oh1 = e_iota == i1
    rest = jnp.where(oh1, -jnp.inf, logits)
    m2 = jnp.max(rest, axis=0, keepdims=True)
    i2 = jnp.min(jnp.where(rest == m2, e_iota, N_EXPERTS), axis=0, keepdims=True)
    oh2 = e_iota == i2
    e2 = jnp.exp(m2 - m1)
    den = 1.0 + e2
    gate_ref[...] = jnp.concatenate([1.0 / den, e2 / den], axis=0)
    eid_ref[...] = jnp.concatenate([i1, i2], axis=0)

    member = oh1.astype(F32) + oh2.astype(F32)
    before = (lax.broadcasted_iota(I32, (tr, tr), 0) <
              lax.broadcasted_iota(I32, (tr, tr), 1)).astype(BF16)
    prefix = jnp.dot(member.astype(BF16), before, preferred_element_type=F32) + run_scr[:, 0:1]
    r1 = jnp.sum(jnp.where(oh1, prefix, 0.0), axis=0, keepdims=True)
    r2 = jnp.sum(jnp.where(oh2, prefix, 0.0), axis=0, keepdims=True)
    rank_ref[...] = jnp.concatenate([r1, r2], axis=0).astype(I32)
    run_scr[...] += jnp.sum(member, axis=1, keepdims=True)
    cnt_ref[...] = run_scr[...].astype(I32)


def _route(x2d, seq, mod, ng, router_w, router_b):
    n, d = x2d.shape
    tr = MOVE_ROWS
    assert seq % tr == 0
    per_seq = seq // tr
    return pl.pallas_call(
        _route_kernel,
        grid=(n // tr,),
        in_specs=[pl.BlockSpec((tr, d), lambda i: (i, 0)),
                  pl.BlockSpec((1, 6, d), lambda i: (i // per_seq, 0, 0)),
                  pl.BlockSpec((4, d), lambda i: (0, 0)),
                  pl.BlockSpec((N_EXPERTS, d), lambda i: (0, 0)),
                  pl.BlockSpec((N_EXPERTS, 1), lambda i: (0, 0))],
        out_specs=[pl.BlockSpec((tr, d), lambda i: (i, 0)),
                   pl.BlockSpec((TOP_K, tr), lambda i: (0, i)),
                   pl.BlockSpec((TOP_K, tr), lambda i: (0, i)),
                   pl.BlockSpec((TOP_K, tr), lambda i: (0, i)),
                   pl.BlockSpec((1, N_EXPERTS, LANES), lambda i: (i, 0, 0)),
                   pl.BlockSpec((N_EXPERTS, LANES), lambda i: (0, 0))],
        out_shape=[jax.ShapeDtypeStruct((n, d), BF16),
                   jax.ShapeDtypeStruct((TOP_K, n), I32),
                   jax.ShapeDtypeStruct((TOP_K, n), F32),
                   jax.ShapeDtypeStruct((TOP_K, n), I32),
                   jax.ShapeDtypeStruct((n // tr, N_EXPERTS, LANES), I32),
                   jax.ShapeDtypeStruct((N_EXPERTS, LANES), I32)],
        scratch_shapes=[pltpu.VMEM((N_EXPERTS, LANES), F32)],
        compiler_params=_cparams(("arbitrary",)),
        name="moe_route",
    )(x2d, mod, ng, router_w.T, router_b.reshape(N_EXPERTS, 1))


def _window_copy(src, dst, sem):
    return pltpu.make_async_copy(src, dst, sem)


def _rows_at(ref, row, n_rows):
    return ref.at[pl.ds(pl.multiple_of(row * SUBLANES, SUBLANES), n_rows * SUBLANES), :]


def _to_tiles(dst, val):
    rows = val.shape[0]
    for jj in range(val.shape[1] // LANES):
        dst[pl.ds(jj, rows, stride=SUBLANES), :] = val[:, jj * LANES:(jj + 1) * LANES]


def _from_tiles(src, rows):
    return jnp.concatenate([src[pl.ds(jj, rows, stride=SUBLANES), :] for jj in range(SUBLANES)],
                           axis=1)


def _dispatch_kernel(start_ref, count_ref, comp_ref, base_ref, ctile_ref, h_ref, eid_ref,
                     rank_ref, hs_ref, win_scr, ovf_scr, zero_scr, sems, sync_sem):
    i = pl.program_id(0)
    n_steps = pl.num_programs(0)
    slot = i % 2
    w = DISPATCH_WINDOW
    t = h_ref.shape[0]
    h = h_ref[...]
    e0 = eid_ref[0:1, :]
    e1 = eid_ref[1:2, :]
    r0 = rank_ref[0:1, :]
    r1 = rank_ref[1:2, :]
    row = lax.broadcasted_iota(I32, (w, t), 0)

    def local_rank(e):
        return jnp.where(e0 == e, r0, jnp.where(e1 == e, r1, -1)) - base_ref[i * N_EXPERTS + e]

    def selector(e, first_row):
        return jnp.where(row + first_row == local_rank(e), 1.0, 0.0).astype(BF16)

    slab = jnp.dot(jnp.concatenate([selector(e, 0) for e in range(N_EXPERTS)], axis=0), h,
                   preferred_element_type=F32)
    for e in range(N_EXPERTS):
        _to_tiles(win_scr.at[slot, e], slab[e * w:(e + 1) * w, :])

    def window(step_slot, e, step):
        dst_row = start_ref[e] + base_ref[step * N_EXPERTS + e]
        return _window_copy(win_scr.at[step_slot, e], _rows_at(hs_ref, dst_row, w),
                            sems.at[step_slot])

    @pl.when(i > 0)
    def _():
        for e in range(N_EXPERTS):
            window(1 - slot, e, i - 1).wait()

    for e in range(N_EXPERTS):
        window(slot, e, i).start()

    for e in range(N_EXPERTS):
        run = ctile_ref[i * N_EXPERTS + e]

        @pl.when(run > w)
        def _():
            def extra(k, c):
                first = k * w
                _to_tiles(ovf_scr, jnp.dot(selector(e, first), h, preferred_element_type=F32))
                dst_row = start_ref[e] + base_ref[i * N_EXPERTS + e] + first
                cp = _window_copy(ovf_scr, _rows_at(hs_ref, dst_row, w), sync_sem)
                cp.start()
                cp.wait()
                return c

            lax.fori_loop(1, (run + w - 1) // w, extra, 0)

    @pl.when(i == n_steps - 1)
    def _():
        for e in range(N_EXPERTS):
            window(slot, e, i).wait()
        zero_scr[...] = jnp.zeros_like(zero_scr)
        tm = zero_scr.shape[0] // SUBLANES

        def zero_fill(row_start):
            cp = _window_copy(zero_scr, _rows_at(hs_ref, row_start, tm), sync_sem)
            cp.start()
            cp.wait()

        for e in range(N_EXPERTS):
            zero_fill(start_ref[e] + count_ref[e])
            zero_fill(start_ref[e] + comp_ref[e])
        used = start_ref[N_EXPERTS - 1] + comp_ref[N_EXPERTS - 1] + tm
        total = hs_ref.shape[0] // SUBLANES

        def tail(k, c):
            zero_fill(used + k * tm)
            return c

        lax.fori_loop(0, (total - used) // tm, tail, 0)


def _dispatch(scalars, h_bf, eid, rank, n_rows):
    n, d = h_bf.shape
    t = DISPATCH_STEPS * MOVE_ROWS
    w = DISPATCH_WINDOW
    assert n % t == 0 and d == SUBLANES * LANES and w % (2 * SUBLANES) == 0 and w <= MOE_ROWS
    grid_spec = pltpu.PrefetchScalarGridSpec(
        num_scalar_prefetch=5,
        grid=(n // t,),
        in_specs=[pl.BlockSpec((t, d), lambda i, *_: (i, 0)),
                  pl.BlockSpec((TOP_K, t), lambda i, *_: (0, i)),
                  pl.BlockSpec((TOP_K, t), lambda i, *_: (0, i))],
        out_specs=pl.BlockSpec(memory_space=pl.ANY),
        scratch_shapes=[pltpu.VMEM((2, N_EXPERTS, w * SUBLANES, LANES), F32),
                        pltpu.VMEM((w * SUBLANES, LANES), F32),
                        pltpu.VMEM((MOE_ROWS * SUBLANES, LANES), F32),
                        pltpu.SemaphoreType.DMA((2,)),
                        pltpu.SemaphoreType.DMA],
    )
    return pl.pallas_call(
        _dispatch_kernel,
        grid_spec=grid_spec,
        out_shape=jax.ShapeDtypeStruct((n_rows * SUBLANES, LANES), F32),
        compiler_params=_cparams(("arbitrary",)),
        name="moe_dispatch",
    )(*scalars, h_bf, eid, rank)


def _expert_kernel(te_ref, tv_ref, tb_ref, hs_ref, wg_ref, wu_ref, wd_ref, y_ref, h_scr, acc_scr):
    del te_ref, tb_ref
    i = pl.program_id(0)
    j = pl.program_id(1)
    last = pl.num_programs(1) - 1
    tm = h_scr.shape[0]

    @pl.when(tv_ref[i] > 0)
    def _():
        @pl.when(j == 0)
        def _():
            h_scr[...] = _from_tiles(hs_ref, tm).astype(BF16)
            acc_scr[...] = jnp.zeros_like(acc_scr)

        acc_scr[...] += _swiglu_chunk(h_scr[...], wg_ref[0], wu_ref[0], wd_ref[0])

        @pl.when(j == last)
        def _():
            _to_tiles(y_ref, acc_scr[...])

    @pl.when(jnp.logical_and(tv_ref[i] == 0, j == last))
    def _():
        y_ref[...] = jnp.zeros_like(y_ref)


def _experts(tile_expert, tile_valid, tile_block, hs, wg, wu, wd):
    d = wg.shape[1]
    f = wg.shape[2]
    tm = MOE_ROWS
    fc = FFN_COLS
    n_tiles = hs.shape[0] // (tm * SUBLANES)
    n_fc = f // fc

    def col(i, j, tv):
        return jnp.where(tv[i] > 0, j, n_fc - 1)

    grid_spec = pltpu.PrefetchScalarGridSpec(
        num_scalar_prefetch=3,
        grid=(n_tiles, n_fc),
        in_specs=[pl.BlockSpec((tm * SUBLANES, LANES), lambda i, j, te, tv, tb: (tb[i], 0)),
                  pl.BlockSpec((1, d, fc), lambda i, j, te, tv, tb: (te[i], 0, col(i, j, tv))),
                  pl.BlockSpec((1, d, fc), lambda i, j, te, tv, tb: (te[i], 0, col(i, j, tv))),
                  pl.BlockSpec((1, fc, d), lambda i, j, te, tv, tb: (te[i], col(i, j, tv), 0))],
        out_specs=pl.BlockSpec((tm * SUBLANES, LANES), lambda i, j, te, tv, tb: (i, 0)),
        scratch_shapes=[pltpu.VMEM((tm, d), BF16), pltpu.VMEM((tm, d), F32)],
    )
    return pl.pallas_call(
        _expert_kernel,
        grid_spec=grid_spec,
        out_shape=jax.ShapeDtypeStruct(hs.shape, F32),
        compiler_params=_cparams(("arbitrary", "arbitrary")),
        name="moe_experts",
    )(tile_expert, tile_valid, tile_block, hs, wg.astype(BF16), wu.astype(BF16), wd.astype(BF16))


def _split(v):
    hi = v.astype(BF16)
    lo = (v - hi.astype(F32)).astype(BF16)
    return hi, lo


def _dot3(q, y):
    qh, ql = _split(q)
    yh, yl = _split(y)
    return (jnp.dot(qh, yh, preferred_element_type=F32)
            + jnp.dot(qh, yl, preferred_element_type=F32)
            + jnp.dot(ql, yh, preferred_element_type=F32))


def _combine_kernel(start_ref, comp_ref, base_ref, ctile_ref, eid_ref, rank_ref, gate_ref, x_ref,
                    mod_ref, ng_ref, y_ref, o_ref, win_scr, ovf_scr, f_scr, sems, sync_sem):
    i = pl.program_id(0)
    n_steps = pl.num_programs(0)
    slot = i % 2
    w = COMBINE_WINDOW
    t = x_ref.shape[0]
    e0 = eid_ref[:, 0:1]
    e1 = eid_ref[:, 1:2]
    r0 = rank_ref[:, 0:1]
    r1 = rank_ref[:, 1:2]
    g0 = gate_ref[:, 0:1]
    g1 = gate_ref[:, 1:2]
    col = lax.broadcasted_iota(I32, (t, w), 1)

    def first_row(step, e, second):
        want = base_ref[step * N_EXPERTS + e] + (w if second else 0)
        return start_ref[e] + jnp.minimum(want, comp_ref[e] - w)

    def window(step_slot, e, step):
        return _window_copy(_rows_at(y_ref, first_row(step, e, False), w),
                            win_scr.at[step_slot, e], sems.at[step_slot])

    @pl.when(i == 0)
    def _():
        for e in range(N_EXPERTS):
            window(slot, e, i).start()

    @pl.when(i + 1 < n_steps)
    def _():
        for e in range(N_EXPERTS):
            window(1 - slot, e, i + 1).start()

    for e in range(N_EXPERTS):
        window(slot, e, i).wait()

    def weights(e, second):
        mine0 = e0 == e
        mine1 = e1 == e
        rank = jnp.where(mine0, r0, jnp.where(mine1, r1, -1))
        gate = jnp.where(mine0, g0, jnp.where(mine1, g1, 0.0))
        local = rank - base_ref[i * N_EXPERTS + e]
        in_window = (local >= w) if second else jnp.logical_and(local >= 0, local < w)
        pos = rank + start_ref[e] - first_row(i, e, second)
        return jnp.where(jnp.logical_and(in_window, pos == col), gate, 0.0)

    f = jnp.zeros((t, x_ref.shape[1]), F32)
    for e in range(0, N_EXPERTS, 2):
        q = jnp.concatenate([weights(e, False), weights(e + 1, False)], axis=1)
        y = jnp.concatenate([_from_tiles(win_scr.at[slot, e], w),
                             _from_tiles(win_scr.at[slot, e + 1], w)], axis=0)
        f = f + _dot3(q, y)
    f_scr[...] = f

    for e in range(N_EXPERTS):
        @pl.when(ctile_ref[i * N_EXPERTS + e] > w)
        def _():
            cp = _window_copy(_rows_at(y_ref, first_row(i, e, True), w), ovf_scr, sync_sem)
            cp.start()
            cp.wait()
            f_scr[...] += _dot3(weights(e, True), _from_tiles(ovf_scr, w))

    g2 = mod_ref[0, 5:6, :]
    o_ref[...] = x_ref[...] + g2 * _rms(f_scr[...], ng_ref[3:4, :])


def _combine(scalars, eid_c, rank_c, gate_c, x2d, seq, mod, ng, y):
    n, d = x2d.shape
    t = MOVE_ROWS
    w = COMBINE_WINDOW
    assert seq % t == 0 and t <= 2 * w and w <= MOE_ROWS
    per_seq = seq // t
    grid_spec = pltpu.PrefetchScalarGridSpec(
        num_scalar_prefetch=4,
        grid=(n // t,),
        in_specs=[pl.BlockSpec((t, TOP_K), lambda i, *_: (i, 0)),
                  pl.BlockSpec((t, TOP_K), lambda i, *_: (i, 0)),
                  pl.BlockSpec((t, TOP_K), lambda i, *_: (i, 0)),
                  pl.BlockSpec((t, d), lambda i, *_: (i, 0)),
                  pl.BlockSpec((1, 6, d), lambda i, *_: (i // per_seq, 0, 0)),
                  pl.BlockSpec((4, d), lambda i, *_: (0, 0)),
                  pl.BlockSpec(memory_space=pl.ANY)],
        out_specs=pl.BlockSpec((t, d), lambda i, *_: (i, 0)),
        scratch_shapes=[pltpu.VMEM((2, N_EXPERTS, w * SUBLANES, LANES), F32),
                        pltpu.VMEM((w * SUBLANES, LANES), F32),
                        pltpu.VMEM((t, d), F32),
                        pltpu.SemaphoreType.DMA((2,)),
                        pltpu.SemaphoreType.DMA],
    )
    return pl.pallas_call(
        _combine_kernel,
        grid_spec=grid_spec,
        out_shape=jax.ShapeDtypeStruct((n, d), F32),
        compiler_params=_cparams(("arbitrary",)),
        name="moe_combine",
    )(*scalars, eid_c, rank_c, gate_c, x2d, mod, ng, y)


def _moe(x2d, seq, mod, ng, router_w, router_b, wg, wu, wd):
    n, d = x2d.shape
    tm = MOE_ROWS
    h_bf, eid, gate, rank, base, cnt = _route(x2d, seq, mod, ng, router_w, router_b)
    counts = cnt[:, 0]
    comp = jnp.maximum(((counts + tm - 1) // tm) * tm, tm)
    ends = jnp.cumsum(comp + tm)
    starts = ends - (comp + tm)
    n_tiles = (TOP_K * n) // tm + 2 * N_EXPERTS
    tile_start = jnp.arange(n_tiles, dtype=I32) * tm
    tile_expert = jnp.minimum(jnp.sum(tile_start[:, None] >= ends[None, :], axis=1),
                              N_EXPERTS - 1).astype(I32)
    tile_valid = jnp.logical_and(tile_start < (starts + comp)[tile_expert],
                                 tile_start < ends[-1]).astype(I32)
    base = base[:, :, 0]

    def runs(step_base):
        nxt = jnp.concatenate([step_base[1:], counts[None, :]], axis=0)
        return step_base.reshape(-1), (nxt - step_base).reshape(-1)

    d_base, d_run = runs(base[::DISPATCH_STEPS])
    c_base, c_run = runs(base)
    hs = _dispatch((starts, counts, comp, d_base, d_run), h_bf, eid, rank, n_tiles * tm)
    tile_block = lax.cummax(jnp.where(tile_valid > 0, jnp.arange(n_tiles, dtype=I32), 0))
    y = _experts(tile_expert, tile_valid, tile_block, hs, wg, wu, wd)
    return _combine((starts, comp, c_base, c_run), eid.T, rank.T, gate.T, x2d, seq, mod, ng, y)


def kernel(x, c, w_ada, b_ada, norm_gain, w_in, b_in, ln_v_gain, ln_v_bias, w_spatial, b_spatial, conv_w, conv_b, ln_conv_gain, ln_conv_bias, group_gain, w_out, ffn_w_gate, ffn_w_up, ffn_w_down, router_w, router_b, moe_w_gate, moe_w_up, moe_w_down):
    bsz, seq, d = x.shape
    depth = w_ada.shape[0]
    mod_all = _ada(c, w_ada, b_ada).reshape(depth, bsz, 6, d)
    for l in range(depth):
        mod = mod_all[l]
        ng = norm_gain[l]
        x = _mix(x, mod, ng, w_in[l], b_in[l], ln_v_gain[l], ln_v_bias[l], w_spatial[l],
                 b_spatial[l], conv_w[l], conv_b[l], ln_conv_gain[l], ln_conv_bias[l],
                 group_gain[l], w_out[l])
        x2d = x.reshape(bsz * seq, d)
        i = l // 2
        if l % 2 == 0:
            x2d = _ffn(x2d, seq, mod, ng, ffn_w_gate[i], ffn_w_up[i], ffn_w_down[i])
        else:
            x2d = _moe(x2d, seq, mod, ng, router_w[i], router_b[i], moe_w_gate[i],
                       moe_w_up[i], moe_w_down[i])
        x = x2d.reshape(bsz, seq, d)
    return x
```

```python
import functools

import jax
import jax.numpy as jnp
from jax import lax
from jax.experimental import pallas as pl
from jax.experimental.pallas import tpu as pltpu

F32 = jnp.float32
BF16 = jnp.bfloat16
I32 = jnp.int32

EPS = 1e-6
CHUNK = 64
GMLP_BLOCK = 128
N_HEADS_A = 8
CONV_WIDTH = 31
N_EXPERTS = 8
TOP_K = 2

LANES = 128
SUBLANES = 8
CONV_HALO = 32
VMEM_LIMIT = 56 * 1024 * 1024

SEQ_TILE = 512
SUB_TILE = 256
FFN_ROWS = 512
FFN_COLS = 1792
MOE_ROWS = 512
MOVE_ROWS = 256
COMBINE_WINDOW = MOVE_ROWS // 2
DISPATCH_STEPS = 2
DISPATCH_WINDOW = 192


def _rms(x, g):
    return x * lax.rsqrt(jnp.mean(x * x, axis=-1, keepdims=True) + EPS) * g


def _layer_norm(x, g, b):
    mu = jnp.mean(x, axis=-1, keepdims=True)
    xc = x - mu
    return xc * lax.rsqrt(jnp.mean(xc * xc, axis=-1, keepdims=True) + EPS) * g + b


_SQRT_2_OVER_PI = 0.7978845608028654


def _sigmoid(x):
    return 0.5 + 0.5 * jnp.tanh(0.5 * x)


def _silu(x):
    return x * _sigmoid(x)


def _gelu(x):
    inner = x * (_SQRT_2_OVER_PI + (_SQRT_2_OVER_PI * 0.044715) * (x * x))
    half = 0.5 * x
    return half + half * jnp.tanh(inner)


def _cparams(sem, vmem=VMEM_LIMIT):
    return pltpu.CompilerParams(dimension_semantics=sem, vmem_limit_bytes=vmem)


def _ada_kernel(c_ref, w_ref, b_ref, o_ref):
    c_act = jax.nn.silu(c_ref[...])
    o_ref[0] = jnp.dot(c_act, w_ref[0], preferred_element_type=F32,
                       precision=lax.Precision.HIGHEST) + b_ref[0]


def _ada(c, w_ada, b_ada):
    depth, d, n6 = w_ada.shape
    bsz = c.shape[0]
    nc = 1536
    return pl.pallas_call(
        _ada_kernel,
        grid=(depth, n6 // nc),
        in_specs=[pl.BlockSpec((bsz, d), lambda l, j: (0, 0)),
                  pl.BlockSpec((1, d, nc), lambda l, j: (l, 0, j)),
                  pl.BlockSpec((1, 1, nc), lambda l, j: (l, 0, j))],
        out_specs=pl.BlockSpec((1, bsz, nc), lambda l, j: (l, 0, j)),
        out_shape=jax.ShapeDtypeStruct((depth, bsz, n6), F32),
        compiler_params=_cparams(("arbitrary", "arbitrary")),
        name="ada_mod",
    )(c, w_ada, b_ada.reshape(depth, 1, n6))


def _mix_kernel(x_ref, mod_ref, ng_ref, w_in_ref, b_in_ref, lnv_g_ref, lnv_b_ref, ws_ref,
                bs_ref, cw_ref, cb_ref, lnc_g_ref, lnc_b_ref, gg_ref, w_out_ref,
                o_ref, wsp_scr, xg_scr, sh_scr, yc_scr):
    ts = x_ref.shape[1]
    d_a = lnv_g_ref.shape[1]
    d_b = lnc_g_ref.shape[1]
    b = pl.program_id(0)
    s = pl.program_id(1)

    @pl.when(jnp.logical_and(b == 0, s == 0))
    def _():
        t_chunk = lax.broadcasted_iota(I32, (GMLP_BLOCK, GMLP_BLOCK), 0) // CHUNK
        s_chunk = lax.broadcasted_iota(I32, (GMLP_BLOCK, GMLP_BLOCK), 1) // CHUNK
        allowed = t_chunk >= s_chunk
        for j in range(N_HEADS_A // 2):
            lo = jnp.where(allowed, ws_ref[2 * j], 0.0).astype(BF16)
            hi = jnp.where(allowed, ws_ref[2 * j + 1], 0.0).astype(BF16)
            wsp_scr[j] = jnp.concatenate([lo, hi], axis=1)

    @pl.when(s == 0)
    def _():
        xg_scr[0:CONV_HALO, :] = jnp.zeros((CONV_HALO, d_b), F32)

    sh1 = mod_ref[0, 0:1, :]
    sc1 = mod_ref[0, 1:2, :]
    g1 = mod_ref[0, 2:3, :]
    head_dim = d_a // N_HEADS_A
    lane = lax.broadcasted_iota(I32, (GMLP_BLOCK, LANES), 1)
    first_head = lane < head_dim
    zero = jnp.zeros((GMLP_BLOCK, LANES), BF16)
    first_tap = CONV_HALO - (CONV_WIDTH - 1)
    keep = CONV_HALO - SUBLANES
    rows = 64
    sub = min(SUB_TILE, ts)

    z_all = []
    for q in range(ts // sub):
        x = x_ref[0, q * sub:(q + 1) * sub, :]
        h = _rms(x, ng_ref[0:1, :]) * (1.0 + sc1) + sh1
        z_all.append(jnp.dot(h.astype(BF16), w_in_ref[...], preferred_element_type=F32)
                     + b_in_ref[...])

    for q in range(ts // sub):
        lo = q * sub
        x = x_ref[0, lo:lo + sub, :]
        z = z_all[q]
        ua = z[:, 0:d_a]
        va = z[:, d_a:2 * d_a]
        ab = z[:, 2 * d_a:2 * d_a + d_b]
        gb = z[:, 2 * d_a + d_b:]

        u = _gelu(ua)
        v = _layer_norm(_gelu(va), lnv_g_ref[...], lnv_b_ref[...]).astype(BF16)
        blocks = []
        for n in range(sub // GMLP_BLOCK):
            cols = []
            for j in range(d_a // LANES):
                vc = v[n * GMLP_BLOCK:(n + 1) * GMLP_BLOCK, j * LANES:(j + 1) * LANES]
                rhs = jnp.concatenate([jnp.where(first_head, vc, zero),
                                       jnp.where(first_head, zero, vc)], axis=0)
                cols.append(jnp.dot(wsp_scr[j], rhs, preferred_element_type=F32))
            blocks.append(jnp.concatenate(cols, axis=1) + bs_ref[...])
        ya = u * jnp.concatenate(blocks, axis=0)

        xg_scr[CONV_HALO + lo:CONV_HALO + lo + sub, :] = ab * _sigmoid(gb)
        new_lo = lo if q == 0 else lo + keep
        new_hi = lo + sub + keep
        for r in range(1, SUBLANES):
            sh_scr[r, new_lo:new_hi, :] = xg_scr[new_lo + r:new_hi + r, :]
        for rc in range(sub // rows):
            base = lo + rc * rows
            for lc in range(d_b // LANES):
                ls = slice(lc * LANES, (lc + 1) * LANES)
                acc = jnp.broadcast_to(cb_ref[:, ls], (rows, LANES))
                for k in range(CONV_WIDTH):
                    off = first_tap + k
                    r = off % SUBLANES
                    r0 = base + off - r
                    if r == 0:
                        win = xg_scr[r0:r0 + rows, ls]
                    else:
                        win = sh_scr[r, r0:r0 + rows, ls]
                    acc = acc + cw_ref[k:k + 1, ls] * win
                yc_scr[base:base + rows, ls] = acc
        yb = _silu(_layer_norm(yc_scr[lo:lo + sub, :], lnc_g_ref[...], lnc_b_ref[...]))

        ycat = jnp.concatenate([_rms(ya, gg_ref[:, 0:d_a]), _rms(yb, gg_ref[:, d_a:])], axis=1)
        y = jnp.dot(ycat.astype(BF16), w_out_ref[...], preferred_element_type=F32)
        o_ref[0, lo:lo + sub, :] = x + g1 * _rms(y, ng_ref[1:2, :])

    xg_scr[0:CONV_HALO, :] = xg_scr[ts:ts + CONV_HALO, :]


def _mix(x, mod, ng, w_in, b_in, lnv_g, lnv_b, w_sp, b_sp, cw, cb, lnc_g, lnc_b, gg, w_out):
    bsz, seq, d = x.shape
    d_in = w_in.shape[1]
    d_a = lnv_g.shape[0]
    d_b = lnc_g.shape[0]
    ts = min(SEQ_TILE, seq)
    assert seq % ts == 0 and ts % GMLP_BLOCK == 0 and ts >= CONV_HALO
    bs_full = jnp.repeat(b_sp.T, d_a // N_HEADS_A, axis=1)
    const = lambda *shape: pl.BlockSpec(shape, lambda b, s: (0,) * len(shape))
    assert ts % min(SUB_TILE, ts) == 0
    return pl.pallas_call(
        _mix_kernel,
        grid=(bsz, seq // ts),
        in_specs=[pl.BlockSpec((1, ts, d), lambda b, s: (b, s, 0)),
                  pl.BlockSpec((1, 6, d), lambda b, s: (b, 0, 0)),
                  const(4, d), const(d, d_in), const(1, d_in), const(1, d_a), const(1, d_a),
                  const(N_HEADS_A, GMLP_BLOCK, GMLP_BLOCK), const(GMLP_BLOCK, d_a),
                  const(CONV_WIDTH, d_b), const(1, d_b), const(1, d_b), const(1, d_b),
                  const(1, d_a + d_b), const(d_a + d_b, d)],
        out_specs=pl.BlockSpec((1, ts, d), lambda b, s: (b, s, 0)),
        out_shape=jax.ShapeDtypeStruct((bsz, seq, d), F32),
        scratch_shapes=[pltpu.VMEM((N_HEADS_A // 2, GMLP_BLOCK, 2 * GMLP_BLOCK), BF16),
                        pltpu.VMEM((ts + CONV_HALO, d_b), F32),
                        pltpu.VMEM((SUBLANES, ts + CONV_HALO, d_b), F32),
                        pltpu.VMEM((ts, d_b), F32)],
        compiler_params=_cparams(("arbitrary", "arbitrary")),
        name="token_mix",
    )(x, mod, ng, w_in.astype(BF16), b_in.reshape(1, d_in), lnv_g.reshape(1, d_a),
      lnv_b.reshape(1, d_a), w_sp, bs_full, cw, cb.reshape(1, d_b), lnc_g.reshape(1, d_b),
      lnc_b.reshape(1, d_b), gg.reshape(1, d_a + d_b), w_out.astype(BF16))


def _swiglu_chunk(h, wg, wu, wd):
    g = jnp.dot(h, wg, preferred_element_type=F32)
    u = jnp.dot(h, wu, preferred_element_type=F32)
    a = (_silu(g) * u).astype(BF16)
    return jnp.dot(a, wd, preferred_element_type=F32)


def _cast_chunks(steps, d, fc):
    for chunks in (8, 4, 2, 1):
        packed_rows = 2 * SUBLANES * chunks
        if N_EXPERTS * chunks <= steps and d % packed_rows == 0 and fc % packed_rows == 0:
            return chunks
    raise ValueError("too few grid steps to convert the expert weights")


def _cast_specs(chunk, chunks, d, fc, step_of):
    last = N_EXPERTS * chunks - 1

    def slab(*grid):
        s = jnp.minimum(step_of(*grid), last)
        return s // chunks, s % chunks

    up = (1, d // chunks, fc)
    down = (1, fc // chunks, d)
    ins = [pl.BlockSpec(up, lambda *g: slab(*g) + (chunk,)),
           pl.BlockSpec(up, lambda *g: slab(*g) + (chunk,)),
           pl.BlockSpec(down, lambda *g: (slab(*g)[0], chunk * chunks + slab(*g)[1], 0))]
    outs = [pl.BlockSpec(up, lambda *g: slab(*g) + (0,)),
            pl.BlockSpec(up, lambda *g: slab(*g) + (0,)),
            pl.BlockSpec(down, lambda *g: slab(*g) + (0,))]
    return ins, outs


def _cast_shapes(d, fc):
    return [jax.ShapeDtypeStruct((N_EXPERTS, d, fc), BF16),
            jax.ShapeDtypeStruct((N_EXPERTS, d, fc), BF16),
            jax.ShapeDtypeStruct((N_EXPERTS, fc, d), BF16)]


def _cast_step(srcs, dsts):
    for src, dst in zip(srcs, dsts):
        dst[...] = src[...].astype(BF16)


def _ffn_kernel(has_cast, x_ref, mod_ref, ng_ref, wg_ref, wu_ref, wd_ref, *rest):
    if has_cast:
        cast_in, (o_ref, *cast_out), (h_scr, acc_scr) = rest[0:3], rest[3:7], rest[7:9]
    else:
        o_ref, h_scr, acc_scr = rest
    j = pl.program_id(1)

    @pl.when(j == 0)
    def _():
        sh2 = mod_ref[0, 3:4, :]
        sc2 = mod_ref[0, 4:5, :]
        h = _rms(x_ref[...], ng_ref[2:3, :]) * (1.0 + sc2) + sh2
        h_scr[...] = h.astype(BF16)
        acc_scr[...] = jnp.zeros_like(acc_scr)

    if has_cast:
        _cast_step(cast_in, cast_out)
    acc_scr[...] += _swiglu_chunk(h_scr[...], wg_ref[...], wu_ref[...], wd_ref[...])

    @pl.when(j == pl.num_programs(1) - 1)
    def _():
        g2 = mod_ref[0, 5:6, :]
        o_ref[...] = x_ref[...] + g2 * _rms(acc_scr[...], ng_ref[3:4, :])


def _ffn(x2d, seq, mod, ng, wg, wu, wd, moe_weights=None):
    n, d = x2d.shape
    f = wg.shape[1]
    tm = min(FFN_ROWS, seq)
    fc = FFN_COLS
    assert seq % tm == 0 and f % fc == 0
    per_seq = seq // tm
    n_fc = f // fc
    in_specs = [pl.BlockSpec((tm, d), lambda i, j: (i, 0)),
                pl.BlockSpec((1, 6, d), lambda i, j: (i // per_seq, 0, 0)),
                pl.BlockSpec((4, d), lambda i, j: (0, 0)),
                pl.BlockSpec((d, fc), lambda i, j: (0, j)),
                pl.BlockSpec((d, fc), lambda i, j: (0, j)),
                pl.BlockSpec((fc, d), lambda i, j: (j, 0))]
    out_specs = [pl.BlockSpec((tm, d), lambda i, j: (i, 0))]
    out_shape = [jax.ShapeDtypeStruct((n, d), F32)]
    operands = [x2d, mod, ng, wg.astype(BF16), wu.astype(BF16), wd.astype(BF16)]
    if moe_weights is not None:
        chunks = _cast_chunks((n // tm) * n_fc, d, fc)
        cast_in, cast_out = _cast_specs(0, chunks, d, fc, lambda i, j: i * n_fc + j)
        in_specs += cast_in
        out_specs += cast_out
        out_shape += _cast_shapes(d, fc)
        operands += list(moe_weights)
    out = pl.pallas_call(
        functools.partial(_ffn_kernel, moe_weights is not None),
        grid=(n // tm, n_fc),
        in_specs=in_specs,
        out_specs=out_specs,
        out_shape=out_shape,
        scratch_shapes=[pltpu.VMEM((tm, d), BF16), pltpu.VMEM((tm, d), F32)],
        compiler_params=_cparams(("arbitrary", "arbitrary")),
        name="ffn_dense",
    )(*operands)
    return out[0], tuple(out[1:])


def _route_kernel(x_ref, mod_ref, ng_ref, rw_ref, rb_ref, h_ref, eid_ref, gate_ref, rank_ref,
                  base_ref, cnt_ref, run_scr):
    i = pl.program_id(0)
    tr = x_ref.shape[0]

    @pl.when(i == 0)
    def _():
        run_scr[...] = jnp.zeros_like(run_scr)

    base_ref[0] = run_scr[...].astype(I32)
    sh2 = mod_ref[0, 3:4, :]
    sc2 = mod_ref[0, 4:5, :]
    h = _rms(x_ref[...], ng_ref[2:3, :]) * (1.0 + sc2) + sh2
    h_ref[...] = h.astype(BF16)
    logits = lax.dot_general(rw_ref[...], h, (((1,), (1,)), ((), ())),
                             preferred_element_type=F32,
                             precision=lax.Precision.HIGHEST) + rb_ref[...]
    e_iota = lax.broadcasted_iota(I32, logits.shape, 0)
    m1 = jnp.max(logits, axis=0, keepdims=True)
    i1 = jnp.min(jnp.where(logits == m1, e_iota, N_EXPERTS), axis=0, keepdims=True)
    oh1 = e_iota == i1
    rest = jnp.where(oh1, -jnp.inf, logits)
    m2 = jnp.max(rest, axis=0, keepdims=True)
    i2 = jnp.min(jnp.where(rest == m2, e_iota, N_EXPERTS), axis=0, keepdims=True)
    oh2 = e_iota == i2
    e2 = jnp.exp(m2 - m1)
    den = 1.0 + e2
    gate_ref[...] = jnp.concatenate([1.0 / den, e2 / den], axis=0)
    eid_ref[...] = jnp.concatenate([i1, i2], axis=0)

    member = oh1.astype(F32) + oh2.astype(F32)
    before = (lax.broadcasted_iota(I32, (tr, tr), 0) <
              lax.broadcasted_iota(I32, (tr, tr), 1)).astype(BF16)
    prefix = jnp.dot(member.astype(BF16), before, preferred_element_type=F32) + run_scr[:, 0:1]
    r1 = jnp.sum(jnp.where(oh1, prefix, 0.0), axis=0, keepdims=True)
    r2 = jnp.sum(jnp.where(oh2, prefix, 0.0), axis=0, keepdims=True)
    rank_ref[...] = jnp.concatenate([r1, r2], axis=0).astype(I32)
    run_scr[...] += jnp.sum(member, axis=1, keepdims=True)
    cnt_ref[...] = run_scr[...].astype(I32)


def _route(x2d, seq, mod, ng, router_w, router_b):
    n, d = x2d.shape
    tr = MOVE_ROWS
    assert seq % tr == 0
    per_seq = seq // tr
    return pl.pallas_call(
        _route_kernel,
        grid=(n // tr,),
        in_specs=[pl.BlockSpec((tr, d), lambda i: (i, 0)),
                  pl.BlockSpec((1, 6, d), lambda i: (i // per_seq, 0, 0)),
                  pl.BlockSpec((4, d), lambda i: (0, 0)),
                  pl.BlockSpec((N_EXPERTS, d), lambda i: (0, 0)),
                  pl.BlockSpec((N_EXPERTS, 1), lambda i: (0, 0))],
        out_specs=[pl.BlockSpec((tr, d), lambda i: (i, 0)),
                   pl.BlockSpec((TOP_K, tr), lambda i: (0, i)),
                   pl.BlockSpec((TOP_K, tr), lambda i: (0, i)),
                   pl.BlockSpec((TOP_K, tr), lambda i: (0, i)),
                   pl.BlockSpec((1, N_EXPERTS, LANES), lambda i: (i, 0, 0)),
                   pl.BlockSpec((N_EXPERTS, LANES), lambda i: (0, 0))],
        out_shape=[jax.ShapeDtypeStruct((n, d), BF16),
                   jax.ShapeDtypeStruct((TOP_K, n), I32),
                   jax.ShapeDtypeStruct((TOP_K, n), F32),
                   jax.ShapeDtypeStruct((TOP_K, n), I32),
                   jax.ShapeDtypeStruct((n // tr, N_EXPERTS, LANES), I32),
                   jax.ShapeDtypeStruct((N_EXPERTS, LANES), I32)],
        scratch_shapes=[pltpu.VMEM((N_EXPERTS, LANES), F32)],
        compiler_params=_cparams(("arbitrary",)),
        name="moe_route",
    )(x2d, mod, ng, router_w.T, router_b.reshape(N_EXPERTS, 1))


def _window_copy(src, dst, sem):
    return pltpu.make_async_copy(src, dst, sem)


def _rows_at(ref, row, n_rows):
    return ref.at[pl.ds(pl.multiple_of(row * SUBLANES, SUBLANES), n_rows * SUBLANES), :]


def _to_tiles(dst, val):
    rows = val.shape[0]
    for jj in range(val.shape[1] // LANES):
        dst[pl.ds(jj, rows, stride=SUBLANES), :] = val[:, jj * LANES:(jj + 1) * LANES]


def _from_tiles(src, rows):
    return jnp.concatenate([src[pl.ds(jj, rows, stride=SUBLANES), :] for jj in range(SUBLANES)],
                           axis=1)


def _dispatch_kernel(start_ref, count_ref, comp_ref, base_ref, ctile_ref, h_ref, eid_ref,
                     rank_ref, hs_ref, win_scr, ovf_scr, zero_scr, sems, sync_sem):
    i = pl.program_id(0)
    n_steps = pl.num_programs(0)
    slot = i % 2
    w = DISPATCH_WINDOW
    t = h_ref.shape[0]
    h = h_ref[...]
    e0 = eid_ref[0:1, :]
    e1 = eid_ref[1:2, :]
    r0 = rank_ref[0:1, :]
    r1 = rank_ref[1:2, :]
    row = lax.broadcasted_iota(I32, (w, t), 0)

    def local_rank(e):
        return jnp.where(e0 == e, r0, jnp.where(e1 == e, r1, -1)) - base_ref[i * N_EXPERTS + e]

    def selector(e, first_row):
        return jnp.where(row + first_row == local_rank(e), 1.0, 0.0).astype(BF16)

    slab = jnp.dot(jnp.concatenate([selector(e, 0) for e in range(N_EXPERTS)], axis=0), h,
                   preferred_element_type=F32)
    for e in range(N_EXPERTS):
        _to_tiles(win_scr.at[slot, e], slab[e * w:(e + 1) * w, :])

    def window(step_slot, e, step):
        dst_row = start_ref[e] + base_ref[step * N_EXPERTS + e]
        return _window_copy(win_scr.at[step_slot, e], _rows_at(hs_ref, dst_row, w),
                            sems.at[step_slot])

    @pl.when(i > 0)
    def _():
        for e in range(N_EXPERTS):
            window(1 - slot, e, i - 1).wait()

    for e in range(N_EXPERTS):
        window(slot, e, i).start()

    for e in range(N_EXPERTS):
        run = ctile_ref[i * N_EXPERTS + e]

        @pl.when(run > w)
        def _():
            def extra(k, c):
                first = k * w
                _to_tiles(ovf_scr, jnp.dot(selector(e, first), h, preferred_element_type=F32))
                dst_row = start_ref[e] + base_ref[i * N_EXPERTS + e] + first
                cp = _window_copy(ovf_scr, _rows_at(hs_ref, dst_row, w), sync_sem)
                cp.start()
                cp.wait()
                return c

            lax.fori_loop(1, (run + w - 1) // w, extra, 0)

    @pl.when(i == n_steps - 1)
    def _():
        for e in range(N_EXPERTS):
            window(slot, e, i).wait()
        zero_scr[...] = jnp.zeros_like(zero_scr)
        tm = zero_scr.shape[0] // SUBLANES

        def zero_fill(row_start):
            return _window_copy(zero_scr, _rows_at(hs_ref, row_start, tm), sync_sem)

        for fills in ([zero_fill(start_ref[e] + count_ref[e]) for e in range(N_EXPERTS)],
                      [zero_fill(start_ref[e] + comp_ref[e]) for e in range(N_EXPERTS)]):
            for cp in fills:
                cp.start()
            for cp in fills:
                cp.wait()
        used = start_ref[N_EXPERTS - 1] + comp_ref[N_EXPERTS - 1] + tm
        n_tail = (hs_ref.shape[0] // SUBLANES - used) // tm

        def tail_start(k, c):
            zero_fill(used + k * tm).start()
            return c

        def tail_wait(k, c):
            zero_fill(used + k * tm).wait()
            return c

        lax.fori_loop(0, n_tail, tail_start, 0)
        lax.fori_loop(0, n_tail, tail_wait, 0)


def _dispatch(scalars, h_bf, eid, rank, n_rows):
    n, d = h_bf.shape
    t = DISPATCH_STEPS * MOVE_ROWS
    w = DISPATCH_WINDOW
    assert n % t == 0 and d == SUBLANES * LANES and w % (2 * SUBLANES) == 0 and w <= MOE_ROWS
    grid_spec = pltpu.PrefetchScalarGridSpec(
        num_scalar_prefetch=5,
        grid=(n // t,),
        in_specs=[pl.BlockSpec((t, d), lambda i, *_: (i, 0)),
                  pl.BlockSpec((TOP_K, t), lambda i, *_: (0, i)),
                  pl.BlockSpec((TOP_K, t), lambda i, *_: (0, i))],
        out_specs=pl.BlockSpec(memory_space=pl.ANY),
        scratch_shapes=[pltpu.VMEM((2, N_EXPERTS, w * SUBLANES, LANES), F32),
                        pltpu.VMEM((w * SUBLANES, LANES), F32),
                        pltpu.VMEM((MOE_ROWS * SUBLANES, LANES), F32),
                        pltpu.SemaphoreType.DMA((2,)),
                        pltpu.SemaphoreType.DMA],
    )
    return pl.pallas_call(
        _dispatch_kernel,
        grid_spec=grid_spec,
        out_shape=jax.ShapeDtypeStruct((n_rows * SUBLANES, LANES), F32),
        compiler_params=_cparams(("arbitrary",)),
        name="moe_dispatch",
    )(*scalars, h_bf, eid, rank)


def _expert_kernel(has_partial, has_cast, te_ref, tv_ref, tb_ref, hs_ref, wg_ref, wu_ref, wd_ref,
                   *rest):
    del te_ref, tb_ref
    rest = list(rest)
    partial_ref = rest.pop(0) if has_partial else None
    cast_in = [rest.pop(0) for _ in range(3)] if has_cast else []
    y_ref, *cast_out = rest
    i = pl.program_id(0)
    tm = hs_ref.shape[0] // SUBLANES

    @pl.when(tv_ref[i] > 0)
    def _():
        _cast_step(cast_in, cast_out)
        h = _from_tiles(hs_ref, tm).astype(BF16)
        part = _swiglu_chunk(h, wg_ref[0], wu_ref[0], wd_ref[0])
        if has_partial:
            part = part + _from_tiles(partial_ref, tm)
        _to_tiles(y_ref, part)

    @pl.when(tv_ref[i] == 0)
    def _():
        _cast_step(cast_in, cast_out)
        y_ref[...] = jnp.zeros_like(y_ref)


def _experts(tile_expert, tile_valid, tile_block, hs, first_chunk, wg, wu, wd):
    d = wg.shape[1]
    f = wg.shape[2]
    tm = MOE_ROWS
    fc = FFN_COLS
    n_tiles = hs.shape[0] // (tm * SUBLANES)
    n_fc = f // fc
    tile_in = pl.BlockSpec((tm * SUBLANES, LANES), lambda i, te, tv, tb: (tb[i], 0))
    weights = tuple(first_chunk)
    y = None
    for c in range(n_fc):
        in_specs = [tile_in,
                    pl.BlockSpec((1, d, fc), lambda i, te, tv, tb: (te[i], 0, 0)),
                    pl.BlockSpec((1, d, fc), lambda i, te, tv, tb: (te[i], 0, 0)),
                    pl.BlockSpec((1, fc, d), lambda i, te, tv, tb: (te[i], 0, 0))]
        operands = [hs, *weights]
        if y is not None:
            in_specs.append(tile_in)
            operands.append(y)
        out_specs = [pl.BlockSpec((tm * SUBLANES, LANES), lambda i, te, tv, tb: (i, 0))]
        out_shape = [jax.ShapeDtypeStruct(hs.shape, F32)]
        if c + 1 < n_fc:
            chunks = _cast_chunks(n_tiles, d, fc)
            cast_in, cast_out = _cast_specs(c + 1, chunks, d, fc, lambda i, *_: i)
            in_specs += cast_in
            operands += [wg, wu, wd]
            out_specs += cast_out
            out_shape += _cast_shapes(d, fc)
        out = pl.pallas_call(
            functools.partial(_expert_kernel, y is not None, c + 1 < n_fc),
            grid_spec=pltpu.PrefetchScalarGridSpec(
                num_scalar_prefetch=3, grid=(n_tiles,), in_specs=in_specs, out_specs=out_specs),
            out_shape=out_shape,
            compiler_params=_cparams(("arbitrary",)),
            name="moe_experts",
        )(tile_expert, tile_valid, tile_block, *operands)
        y, weights = out[0], tuple(out[1:])
    return y


def _split(v):
    hi = v.astype(BF16)
    lo = (v - hi.astype(F32)).astype(BF16)
    return hi, lo


def _dot3(q, y):
    qh, ql = _split(q)
    yh, yl = _split(y)
    return (jnp.dot(qh, yh, preferred_element_type=F32)
            + jnp.dot(qh, yl, preferred_element_type=F32)
            + jnp.dot(ql, yh, preferred_element_type=F32))


def _combine_kernel(start_ref, comp_ref, base_ref, ctile_ref, eid_ref, rank_ref, gate_ref, x_ref,
                    mod_ref, ng_ref, y_ref, o_ref, win_scr, ovf_scr, f_scr, sems, sync_sem):
    i = pl.program_id(0)
    n_steps = pl.num_programs(0)
    slot = i % 2
    w = COMBINE_WINDOW
    t = x_ref.shape[0]
    e0 = eid_ref[:, 0:1]
    e1 = eid_ref[:, 1:2]
    r0 = rank_ref[:, 0:1]
    r1 = rank_ref[:, 1:2]
    g0 = gate_ref[:, 0:1]
    g1 = gate_ref[:, 1:2]
    col = lax.broadcasted_iota(I32, (t, w), 1)

    def first_row(step, e, second):
        want = base_ref[step * N_EXPERTS + e] + (w if second else 0)
        return start_ref[e] + jnp.minimum(want, comp_ref[e] - w)

    def window(step_slot, e, step):
        return _window_copy(_rows_at(y_ref, first_row(step, e, False), w),
                            win_scr.at[step_slot, e], sems.at[step_slot])

    @pl.when(i == 0)
    def _():
        for e in range(N_EXPERTS):
            window(slot, e, i).start()

    @pl.when(i + 1 < n_steps)
    def _():
        for e in range(N_EXPERTS):
            window(1 - slot, e, i + 1).start()

    for e in range(N_EXPERTS):
        window(slot, e, i).wait()

    def weights(e, second):
        mine0 = e0 == e
        mine1 = e1 == e
        rank = jnp.where(mine0, r0, jnp.where(mine1, r1, -1))
        gate = jnp.where(mine0, g0, jnp.where(mine1, g1, 0.0))
        local = rank - base_ref[i * N_EXPERTS + e]
        in_window = (local >= w) if second else jnp.logical_and(local >= 0, local < w)
        pos = rank + start_ref[e] - first_row(i, e, second)
        return jnp.where(jnp.logical_and(in_window, pos == col), gate, 0.0)

    f = jnp.zeros((t, x_ref.shape[1]), F32)
    for e in range(0, N_EXPERTS, 2):
        q = jnp.concatenate([weights(e, False), weights(e + 1, False)], axis=1)
        y = jnp.concatenate([_from_tiles(win_scr.at[slot, e], w),
                             _from_tiles(win_scr.at[slot, e + 1], w)], axis=0)
        f = f + _dot3(q, y)
    f_scr[...] = f

    for e in range(N_EXPERTS):
        @pl.when(ctile_ref[i * N_EXPERTS + e] > w)
        def _():
            cp = _window_copy(_rows_at(y_ref, first_row(i, e, True), w), ovf_scr, sync_sem)
            cp.start()
            cp.wait()
            f_scr[...] += _dot3(weights(e, True), _from_tiles(ovf_scr, w))

    g2 = mod_ref[0, 5:6, :]
    o_ref[...] = x_ref[...] + g2 * _rms(f_scr[...], ng_ref[3:4, :])


def _combine(scalars, eid_c, rank_c, gate_c, x2d, seq, mod, ng, y):
    n, d = x2d.shape
    t = MOVE_ROWS
    w = COMBINE_WINDOW
    assert seq % t == 0 and t <= 2 * w and w <= MOE_ROWS
    per_seq = seq // t
    grid_spec = pltpu.PrefetchScalarGridSpec(
        num_scalar_prefetch=4,
        grid=(n // t,),
        in_specs=[pl.BlockSpec((t, TOP_K), lambda i, *_: (i, 0)),
                  pl.BlockSpec((t, TOP_K), lambda i, *_: (i, 0)),
                  pl.BlockSpec((t, TOP_K), lambda i, *_: (i, 0)),
                  pl.BlockSpec((t, d), lambda i, *_: (i, 0)),
                  pl.BlockSpec((1, 6, d), lambda i, *_: (i // per_seq, 0, 0)),
                  pl.BlockSpec((4, d), lambda i, *_: (0, 0)),
                  pl.BlockSpec(memory_space=pl.ANY)],
        out_specs=pl.BlockSpec((t, d), lambda i, *_: (i, 0)),
        scratch_shapes=[pltpu.VMEM((2, N_EXPERTS, w * SUBLANES, LANES), F32),
                        pltpu.VMEM((w * SUBLANES, LANES), F32),
                        pltpu.VMEM((t, d), F32),
                        pltpu.SemaphoreType.DMA((2,)),
                        pltpu.SemaphoreType.DMA],
    )
    return pl.pallas_call(
        _combine_kernel,
        grid_spec=grid_spec,
        out_shape=jax.ShapeDtypeStruct((n, d), F32),
        compiler_params=_cparams(("arbitrary",)),
        name="moe_combine",
    )(*scalars, eid_c, rank_c, gate_c, x2d, mod, ng, y)


def _moe(x2d, seq, mod, ng, router_w, router_b, first_chunk, wg, wu, wd):
    n, d = x2d.shape
    tm = MOE_ROWS
    h_bf, eid, gate, rank, base, cnt = _route(x2d, seq, mod, ng, router_w, router_b)
    counts = cnt[:, 0]
    comp = jnp.maximum(((counts + tm - 1) // tm) * tm, tm)
    ends = jnp.cumsum(comp + tm)
    starts = ends - (comp + tm)
    n_tiles = (TOP_K * n) // tm + 2 * N_EXPERTS
    tile_start = jnp.arange(n_tiles, dtype=I32) * tm
    tile_expert = jnp.minimum(jnp.sum(tile_start[:, None] >= ends[None, :], axis=1),
                              N_EXPERTS - 1).astype(I32)
    tile_valid = jnp.logical_and(tile_start < (starts + comp)[tile_expert],
                                 tile_start < ends[-1]).astype(I32)
    base = base[:, :, 0]

    def runs(step_base):
        nxt = jnp.concatenate([step_base[1:], counts[None, :]], axis=0)
        return step_base.reshape(-1), (nxt - step_base).reshape(-1)

    d_base, d_run = runs(base[::DISPATCH_STEPS])
    c_base, c_run = runs(base)
    hs = _dispatch((starts, counts, comp, d_base, d_run), h_bf, eid, rank, n_tiles * tm)
    tile_block = lax.cummax(jnp.where(tile_valid > 0, jnp.arange(n_tiles, dtype=I32), 0))
    y = _experts(tile_expert, tile_valid, tile_block, hs, first_chunk, wg, wu, wd)
    return _combine((starts, comp, c_base, c_run), eid.T, rank.T, gate.T, x2d, seq, mod, ng, y)


def kernel(x, c, w_ada, b_ada, norm_gain, w_in, b_in, ln_v_gain, ln_v_bias, w_spatial, b_spatial, conv_w, conv_b, ln_conv_gain, ln_conv_bias, group_gain, w_out, ffn_w_gate, ffn_w_up, ffn_w_down, router_w, router_b, moe_w_gate, moe_w_up, moe_w_down):
    bsz, seq, d = x.shape
    depth = w_ada.shape[0]
    mod_all = _ada(c, w_ada, b_ada).reshape(depth, bsz, 6, d)
    for l in range(depth):
        mod = mod_all[l]
        ng = norm_gain[l]
        x = _mix(x, mod, ng, w_in[l], b_in[l], ln_v_gain[l], ln_v_bias[l], w_spatial[l],
                 b_spatial[l], conv_w[l], conv_b[l], ln_conv_gain[l], ln_conv_bias[l],
                 group_gain[l], w_out[l])
        x2d = x.reshape(bsz * seq, d)
        i = l // 2
        if l % 2 == 0:
            moe = (moe_w_gate[i], moe_w_up[i], moe_w_down[i]) if l + 1 < depth else None
            x2d, first_chunk = _ffn(x2d, seq, mod, ng, ffn_w_gate[i], ffn_w_up[i], ffn_w_down[i],
                                    moe)
        else:
            x2d = _moe(x2d, seq, mod, ng, router_w[i], router_b[i], first_chunk, moe_w_gate[i],
                       moe_w_up[i], moe_w_down[i])
        x = x2d.reshape(bsz, seq, d)
    return x
```

```python
import functools

import jax
import jax.numpy as jnp
from jax import lax
from jax.experimental import pallas as pl
from jax.experimental.pallas import tpu as pltpu

F32 = jnp.float32
BF16 = jnp.bfloat16
I32 = jnp.int32

EPS = 1e-6
CHUNK = 64
GMLP_BLOCK = 128
N_HEADS_A = 8
CONV_WIDTH = 31
N_EXPERTS = 8
TOP_K = 2

LANES = 128
SUBLANES = 8
CONV_HALO = 32
VMEM_LIMIT = 56 * 1024 * 1024

SEQ_TILE = 512
SUB_TILE = 256
FFN_ROWS = 512
FFN_COLS = 1792
MOE_ROWS = 512
MOVE_ROWS = 256
COMBINE_WINDOW = MOVE_ROWS // 2
DISPATCH_STEPS = 2
DISPATCH_WINDOW = 160
WINDOW_PARTS = 4


def _rms(x, g):
    return x * lax.rsqrt(jnp.mean(x * x, axis=-1, keepdims=True) + EPS) * g


def _layer_norm(x, g, b):
    mu = jnp.mean(x, axis=-1, keepdims=True)
    xc = x - mu
    return xc * lax.rsqrt(jnp.mean(xc * xc, axis=-1, keepdims=True) + EPS) * g + b


_SQRT_2_OVER_PI = 0.7978845608028654


def _sigmoid(x):
    return 0.5 + 0.5 * jnp.tanh(0.5 * x)


def _silu(x):
    return x * _sigmoid(x)


def _gelu(x):
    inner = x * (_SQRT_2_OVER_PI + (_SQRT_2_OVER_PI * 0.044715) * (x * x))
    half = 0.5 * x
    return half + half * jnp.tanh(inner)


def _cparams(sem, vmem=VMEM_LIMIT):
    return pltpu.CompilerParams(dimension_semantics=sem, vmem_limit_bytes=vmem)


def _ada_kernel(c_ref, w_ref, b_ref, o_ref):
    c_act = jax.nn.silu(c_ref[...])
    o_ref[0] = jnp.dot(c_act, w_ref[0], preferred_element_type=F32,
                       precision=lax.Precision.HIGHEST) + b_ref[0]


def _ada(c, w_ada, b_ada):
    depth, d, n6 = w_ada.shape
    bsz = c.shape[0]
    nc = 1536
    return pl.pallas_call(
        _ada_kernel,
        grid=(depth, n6 // nc),
        in_specs=[pl.BlockSpec((bsz, d), lambda l, j: (0, 0)),
                  pl.BlockSpec((1, d, nc), lambda l, j: (l, 0, j)),
                  pl.BlockSpec((1, 1, nc), lambda l, j: (l, 0, j))],
        out_specs=pl.BlockSpec((1, bsz, nc), lambda l, j: (l, 0, j)),
        out_shape=jax.ShapeDtypeStruct((depth, bsz, n6), F32),
        compiler_params=_cparams(("arbitrary", "arbitrary")),
        name="ada_mod",
    )(c, w_ada, b_ada.reshape(depth, 1, n6))


def _mix_kernel(x_ref, mod_ref, ng_ref, w_in_ref, b_in_ref, lnv_g_ref, lnv_b_ref, ws_ref,
                bs_ref, cw_ref, cb_ref, lnc_g_ref, lnc_b_ref, gg_ref, w_out_ref,
                o_ref, wsp_scr, xg_scr, sh_scr, yc_scr):
    ts = x_ref.shape[1]
    d_a = lnv_g_ref.shape[1]
    d_b = lnc_g_ref.shape[1]
    b = pl.program_id(0)
    s = pl.program_id(1)

    @pl.when(jnp.logical_and(b == 0, s == 0))
    def _():
        t_chunk = lax.broadcasted_iota(I32, (GMLP_BLOCK, GMLP_BLOCK), 0) // CHUNK
        s_chunk = lax.broadcasted_iota(I32, (GMLP_BLOCK, GMLP_BLOCK), 1) // CHUNK
        allowed = t_chunk >= s_chunk
        for j in range(N_HEADS_A // 2):
            lo = jnp.where(allowed, ws_ref[2 * j], 0.0).astype(BF16)
            hi = jnp.where(allowed, ws_ref[2 * j + 1], 0.0).astype(BF16)
            wsp_scr[j] = jnp.concatenate([lo, hi], axis=1)

    @pl.when(s == 0)
    def _():
        xg_scr[0:CONV_HALO, :] = jnp.zeros((CONV_HALO, d_b), F32)

    sh1 = mod_ref[0, 0:1, :]
    sc1 = mod_ref[0, 1:2, :]
    g1 = mod_ref[0, 2:3, :]
    head_dim = d_a // N_HEADS_A
    lane = lax.broadcasted_iota(I32, (GMLP_BLOCK, LANES), 1)
    first_head = lane < head_dim
    zero = jnp.zeros((GMLP_BLOCK, LANES), BF16)
    first_tap = CONV_HALO - (CONV_WIDTH - 1)
    keep = CONV_HALO - SUBLANES
    rows = 64
    sub = min(SUB_TILE, ts)

    z_all = []
    for q in range(ts // sub):
        x = x_ref[0, q * sub:(q + 1) * sub, :]
        h = _rms(x, ng_ref[0:1, :]) * (1.0 + sc1) + sh1
        z_all.append(jnp.dot(h.astype(BF16), w_in_ref[...], preferred_element_type=F32)
                     + b_in_ref[...])

    for q in range(ts // sub):
        lo = q * sub
        x = x_ref[0, lo:lo + sub, :]
        z = z_all[q]
        ua = z[:, 0:d_a]
        va = z[:, d_a:2 * d_a]
        ab = z[:, 2 * d_a:2 * d_a + d_b]
        gb = z[:, 2 * d_a + d_b:]

        u = _gelu(ua)
        v = _layer_norm(_gelu(va), lnv_g_ref[...], lnv_b_ref[...]).astype(BF16)
        blocks = []
        for n in range(sub // GMLP_BLOCK):
            cols = []
            for j in range(d_a // LANES):
                vc = v[n * GMLP_BLOCK:(n + 1) * GMLP_BLOCK, j * LANES:(j + 1) * LANES]
                rhs = jnp.concatenate([jnp.where(first_head, vc, zero),
                                       jnp.where(first_head, zero, vc)], axis=0)
                cols.append(jnp.dot(wsp_scr[j], rhs, preferred_element_type=F32))
            blocks.append(jnp.concatenate(cols, axis=1) + bs_ref[...])
        ya = u * jnp.concatenate(blocks, axis=0)

        xg_scr[CONV_HALO + lo:CONV_HALO + lo + sub, :] = ab * _sigmoid(gb)
        new_lo = lo if q == 0 else lo + keep
        new_hi = lo + sub + keep
        for r in range(1, SUBLANES):
            sh_scr[r, new_lo:new_hi, :] = xg_scr[new_lo + r:new_hi + r, :]
        for rc in range(sub // rows):
            base = lo + rc * rows
            for lc in range(d_b // LANES):
                ls = slice(lc * LANES, (lc + 1) * LANES)
                acc = jnp.broadcast_to(cb_ref[:, ls], (rows, LANES))
                for k in range(CONV_WIDTH):
                    off = first_tap + k
                    r = off % SUBLANES
                    r0 = base + off - r
                    if r == 0:
                        win = xg_scr[r0:r0 + rows, ls]
                    else:
                        win = sh_scr[r, r0:r0 + rows, ls]
                    acc = acc + cw_ref[k:k + 1, ls] * win
                yc_scr[base:base + rows, ls] = acc
        yb = _silu(_layer_norm(yc_scr[lo:lo + sub, :], lnc_g_ref[...], lnc_b_ref[...]))

        ycat = jnp.concatenate([_rms(ya, gg_ref[:, 0:d_a]), _rms(yb, gg_ref[:, d_a:])], axis=1)
        y = jnp.dot(ycat.astype(BF16), w_out_ref[...], preferred_element_type=F32)
        o_ref[0, lo:lo + sub, :] = x + g1 * _rms(y, ng_ref[1:2, :])

    xg_scr[0:CONV_HALO, :] = xg_scr[ts:ts + CONV_HALO, :]


def _mix(x, mod, ng, w_in, b_in, lnv_g, lnv_b, w_sp, b_sp, cw, cb, lnc_g, lnc_b, gg, w_out):
    bsz, seq, d = x.shape
    d_in = w_in.shape[1]
    d_a = lnv_g.shape[0]
    d_b = lnc_g.shape[0]
    ts = min(SEQ_TILE, seq)
    assert seq % ts == 0 and ts % GMLP_BLOCK == 0 and ts >= CONV_HALO
    bs_full = jnp.repeat(b_sp.T, d_a // N_HEADS_A, axis=1)
    const = lambda *shape: pl.BlockSpec(shape, lambda b, s: (0,) * len(shape))
    assert ts % min(SUB_TILE, ts) == 0
    return pl.pallas_call(
        _mix_kernel,
        grid=(bsz, seq // ts),
        in_specs=[pl.BlockSpec((1, ts, d), lambda b, s: (b, s, 0)),
                  pl.BlockSpec((1, 6, d), lambda b, s: (b, 0, 0)),
                  const(4, d), const(d, d_in), const(1, d_in), const(1, d_a), const(1, d_a),
                  const(N_HEADS_A, GMLP_BLOCK, GMLP_BLOCK), const(GMLP_BLOCK, d_a),
                  const(CONV_WIDTH, d_b), const(1, d_b), const(1, d_b), const(1, d_b),
                  const(1, d_a + d_b), const(d_a + d_b, d)],
        out_specs=pl.BlockSpec((1, ts, d), lambda b, s: (b, s, 0)),
        out_shape=jax.ShapeDtypeStruct((bsz, seq, d), F32),
        scratch_shapes=[pltpu.VMEM((N_HEADS_A // 2, GMLP_BLOCK, 2 * GMLP_BLOCK), BF16),
                        pltpu.VMEM((ts + CONV_HALO, d_b), F32),
                        pltpu.VMEM((SUBLANES, ts + CONV_HALO, d_b), F32),
                        pltpu.VMEM((ts, d_b), F32)],
        compiler_params=_cparams(("arbitrary", "arbitrary")),
        name="token_mix",
    )(x, mod, ng, w_in.astype(BF16), b_in.reshape(1, d_in), lnv_g.reshape(1, d_a),
      lnv_b.reshape(1, d_a), w_sp, bs_full, cw, cb.reshape(1, d_b), lnc_g.reshape(1, d_b),
      lnc_b.reshape(1, d_b), gg.reshape(1, d_a + d_b), w_out.astype(BF16))


def _swiglu_chunk(h, wg, wu, wd):
    g = jnp.dot(h, wg, preferred_element_type=F32)
    u = jnp.dot(h, wu, preferred_element_type=F32)
    a = (_silu(g) * u).astype(BF16)
    return jnp.dot(a, wd, preferred_element_type=F32)


def _cast_chunks(steps, d, fc):
    for chunks in (8, 4, 2, 1):
        packed_rows = 2 * SUBLANES * chunks
        if N_EXPERTS * chunks <= steps and d % packed_rows == 0 and fc % packed_rows == 0:
            return chunks
    raise ValueError("too few grid steps to convert the expert weights")


def _cast_specs(chunk, chunks, d, fc, step_of):
    last = N_EXPERTS * chunks - 1

    def slab(*grid):
        s = jnp.minimum(step_of(*grid), last)
        return s // chunks, s % chunks

    up = (1, d // chunks, fc)
    down = (1, fc // chunks, d)
    ins = [pl.BlockSpec(up, lambda *g: slab(*g) + (chunk,)),
           pl.BlockSpec(up, lambda *g: slab(*g) + (chunk,)),
           pl.BlockSpec(down, lambda *g: (slab(*g)[0], chunk * chunks + slab(*g)[1], 0))]
    outs = [pl.BlockSpec(up, lambda *g: slab(*g) + (0,)),
            pl.BlockSpec(up, lambda *g: slab(*g) + (0,)),
            pl.BlockSpec(down, lambda *g: slab(*g) + (0,))]
    return ins, outs


def _cast_shapes(d, fc):
    return [jax.ShapeDtypeStruct((N_EXPERTS, d, fc), BF16),
            jax.ShapeDtypeStruct((N_EXPERTS, d, fc), BF16),
            jax.ShapeDtypeStruct((N_EXPERTS, fc, d), BF16)]


def _cast_step(srcs, dsts):
    for src, dst in zip(srcs, dsts):
        dst[...] = src[...].astype(BF16)


def _ffn_kernel(has_cast, x_ref, mod_ref, ng_ref, wg_ref, wu_ref, wd_ref, *rest):
    if has_cast:
        cast_in, (o_ref, *cast_out), (h_scr, acc_scr) = rest[0:3], rest[3:7], rest[7:9]
    else:
        o_ref, h_scr, acc_scr = rest
    j = pl.program_id(1)

    @pl.when(j == 0)
    def _():
        sh2 = mod_ref[0, 3:4, :]
        sc2 = mod_ref[0, 4:5, :]
        h = _rms(x_ref[...], ng_ref[2:3, :]) * (1.0 + sc2) + sh2
        h_scr[...] = h.astype(BF16)
        acc_scr[...] = jnp.zeros_like(acc_scr)

    if has_cast:
        _cast_step(cast_in, cast_out)
    acc_scr[...] += _swiglu_chunk(h_scr[...], wg_ref[...], wu_ref[...], wd_ref[...])

    @pl.when(j == pl.num_programs(1) - 1)
    def _():
        g2 = mod_ref[0, 5:6, :]
        o_ref[...] = x_ref[...] + g2 * _rms(acc_scr[...], ng_ref[3:4, :])


def _ffn(x2d, seq, mod, ng, wg, wu, wd, moe_weights=None):
    n, d = x2d.shape
    f = wg.shape[1]
    tm = min(FFN_ROWS, seq)
    fc = FFN_COLS
    assert seq % tm == 0 and f % fc == 0
    per_seq = seq // tm
    n_fc = f // fc
    in_specs = [pl.BlockSpec((tm, d), lambda i, j: (i, 0)),
                pl.BlockSpec((1, 6, d), lambda i, j: (i // per_seq, 0, 0)),
                pl.BlockSpec((4, d), lambda i, j: (0, 0)),
                pl.BlockSpec((d, fc), lambda i, j: (0, j)),
                pl.BlockSpec((d, fc), lambda i, j: (0, j)),
                pl.BlockSpec((fc, d), lambda i, j: (j, 0))]
    out_specs = [pl.BlockSpec((tm, d), lambda i, j: (i, 0))]
    out_shape = [jax.ShapeDtypeStruct((n, d), F32)]
    operands = [x2d, mod, ng, wg.astype(BF16), wu.astype(BF16), wd.astype(BF16)]
    if moe_weights is not None:
        chunks = _cast_chunks((n // tm) * n_fc, d, fc)
        cast_in, cast_out = _cast_specs(0, chunks, d, fc, lambda i, j: i * n_fc + j)
        in_specs += cast_in
        out_specs += cast_out
        out_shape += _cast_shapes(d, fc)
        operands += list(moe_weights)
    out = pl.pallas_call(
        functools.partial(_ffn_kernel, moe_weights is not None),
        grid=(n // tm, n_fc),
        in_specs=in_specs,
        out_specs=out_specs,
        out_shape=out_shape,
        scratch_shapes=[pltpu.VMEM((tm, d), BF16), pltpu.VMEM((tm, d), F32)],
        compiler_params=_cparams(("arbitrary", "arbitrary")),
        name="ffn_dense",
    )(*operands)
    return out[0], tuple(out[1:])


def _route_kernel(x_ref, mod_ref, ng_ref, rw_ref, rb_ref, h_ref, eid_ref, gate_ref, rank_ref,
                  base_ref, cnt_ref, run_scr):
    i = pl.program_id(0)
    tr = x_ref.shape[0]

    @pl.when(i == 0)
    def _():
        run_scr[...] = jnp.zeros_like(run_scr)

    base_ref[0] = run_scr[...].astype(I32)
    sh2 = mod_ref[0, 3:4, :]
    sc2 = mod_ref[0, 4:5, :]
    h = _rms(x_ref[...], ng_ref[2:3, :]) * (1.0 + sc2) + sh2
    h_ref[...] = h.astype(BF16)
    logits = lax.dot_general(rw_ref[...], h, (((1,), (1,)), ((), ())),
                             preferred_element_type=F32,
                             precision=lax.Precision.HIGHEST) + rb_ref[...]
    e_iota = lax.broadcasted_iota(I32, logits.shape, 0)
    m1 = jnp.max(logits, axis=0, keepdims=True)
    i1 = jnp.min(jnp.where(logits == m1, e_iota, N_EXPERTS), axis=0, keepdims=True)
    oh1 = e_iota == i1
    rest = jnp.where(oh1, -jnp.inf, logits)
    m2 = jnp.max(rest, axis=0, keepdims=True)
    i2 = jnp.min(jnp.where(rest == m2, e_iota, N_EXPERTS), axis=0, keepdims=True)
    oh2 = e_iota == i2
    e2 = jnp.exp(m2 - m1)
    den = 1.0 + e2
    gate_ref[...] = jnp.concatenate([1.0 / den, e2 / den], axis=0)
    eid_ref[...] = jnp.concatenate([i1, i2], axis=0)

    member = oh1.astype(F32) + oh2.astype(F32)
    before = (lax.broadcasted_iota(I32, (tr, tr), 0) <
              lax.broadcasted_iota(I32, (tr, tr), 1)).astype(BF16)
    prefix = jnp.dot(member.astype(BF16), before, preferred_element_type=F32) + run_scr[:, 0:1]
    r1 = jnp.sum(jnp.where(oh1, prefix, 0.0), axis=0, keepdims=True)
    r2 = jnp.sum(jnp.where(oh2, prefix, 0.0), axis=0, keepdims=True)
    rank_ref[...] = jnp.concatenate([r1, r2], axis=0).astype(I32)
    run_scr[...] += jnp.sum(member, axis=1, keepdims=True)
    cnt_ref[...] = run_scr[...].astype(I32)


def _route(x2d, seq, mod, ng, router_w, router_b):
    n, d = x2d.shape
    tr = MOVE_ROWS
    assert seq % tr == 0
    per_seq = seq // tr
    return pl.pallas_call(
        _route_kernel,
        grid=(n // tr,),
        in_specs=[pl.BlockSpec((tr, d), lambda i: (i, 0)),
                  pl.BlockSpec((1, 6, d), lambda i: (i // per_seq, 0, 0)),
                  pl.BlockSpec((4, d), lambda i: (0, 0)),
                  pl.BlockSpec((N_EXPERTS, d), lambda i: (0, 0)),
                  pl.BlockSpec((N_EXPERTS, 1), lambda i: (0, 0))],
        out_specs=[pl.BlockSpec((tr, d), lambda i: (i, 0)),
                   pl.BlockSpec((TOP_K, tr), lambda i: (0, i)),
                   pl.BlockSpec((TOP_K, tr), lambda i: (0, i)),
                   pl.BlockSpec((TOP_K, tr), lambda i: (0, i)),
                   pl.BlockSpec((1, N_EXPERTS, LANES), lambda i: (i, 0, 0)),
                   pl.BlockSpec((N_EXPERTS, LANES), lambda i: (0, 0))],
        out_shape=[jax.ShapeDtypeStruct((n, d), BF16),
                   jax.ShapeDtypeStruct((TOP_K, n), I32),
                   jax.ShapeDtypeStruct((TOP_K, n), F32),
                   jax.ShapeDtypeStruct((TOP_K, n), I32),
                   jax.ShapeDtypeStruct((n // tr, N_EXPERTS, LANES), I32),
                   jax.ShapeDtypeStruct((N_EXPERTS, LANES), I32)],
        scratch_shapes=[pltpu.VMEM((N_EXPERTS, LANES), F32)],
        compiler_params=_cparams(("arbitrary",)),
        name="moe_route",
    )(x2d, mod, ng, router_w.T, router_b.reshape(N_EXPERTS, 1))


def _window_copy(src, dst, sem):
    return pltpu.make_async_copy(src, dst, sem)


def _rows_at(ref, row, n_rows):
    return ref.at[pl.ds(pl.multiple_of(row * SUBLANES, SUBLANES), n_rows * SUBLANES), :]


def _to_tiles(dst, val):
    rows = val.shape[0]
    for jj in range(val.shape[1] // LANES):
        dst[pl.ds(jj, rows, stride=SUBLANES), :] = val[:, jj * LANES:(jj + 1) * LANES]


def _from_tiles(src, rows):
    return jnp.concatenate([src[pl.ds(jj, rows, stride=SUBLANES), :] for jj in range(SUBLANES)],
                           axis=1)


def _dispatch_kernel(start_ref, count_ref, comp_ref, base_ref, ctile_ref, h_ref, eid_ref,
                     rank_ref, hs_ref, win_scr, ovf_scr, zero_scr, sems, sync_sem):
    i = pl.program_id(0)
    n_steps = pl.num_programs(0)
    slot = i % 2
    w = DISPATCH_WINDOW
    t = h_ref.shape[0]
    h = h_ref[...]
    e0 = eid_ref[0:1, :]
    e1 = eid_ref[1:2, :]
    r0 = rank_ref[0:1, :]
    r1 = rank_ref[1:2, :]
    row = lax.broadcasted_iota(I32, (w, t), 0)

    def local_rank(e):
        return jnp.where(e0 == e, r0, jnp.where(e1 == e, r1, -1)) - base_ref[i * N_EXPERTS + e]

    def selector(e, first_row):
        return jnp.where(row + first_row == local_rank(e), 1.0, 0.0).astype(BF16)

    slab = jnp.dot(jnp.concatenate([selector(e, 0) for e in range(N_EXPERTS)], axis=0), h,
                   preferred_element_type=F32)
    for e in range(N_EXPERTS):
        _to_tiles(win_scr.at[slot, e], slab[e * w:(e + 1) * w, :])

    def window(step_slot, e, step):
        dst_row = start_ref[e] + base_ref[step * N_EXPERTS + e]
        part = w // WINDOW_PARTS
        return [_window_copy(win_scr.at[step_slot, e, pl.ds(p * part * SUBLANES, part * SUBLANES), :],
                             _rows_at(hs_ref, dst_row + p * part, part), sems.at[step_slot])
                for p in range(WINDOW_PARTS)]

    @pl.when(i > 0)
    def _():
        for e in range(N_EXPERTS):
            for cp in window(1 - slot, e, i - 1):
                cp.wait()

    for e in range(N_EXPERTS):
        for p, cp in enumerate(window(slot, e, i)):
            cp.start(priority=p % 2)

    for e in range(N_EXPERTS):
        run = ctile_ref[i * N_EXPERTS + e]

        @pl.when(run > w)
        def _():
            def extra(k, c):
                first = k * w
                _to_tiles(ovf_scr, jnp.dot(selector(e, first), h, preferred_element_type=F32))
                dst_row = start_ref[e] + base_ref[i * N_EXPERTS + e] + first
                cp = _window_copy(ovf_scr, _rows_at(hs_ref, dst_row, w), sync_sem)
                cp.start()
                cp.wait()
                return c

            lax.fori_loop(1, (run + w - 1) // w, extra, 0)

    @pl.when(i == n_steps - 1)
    def _():
        for e in range(N_EXPERTS):
            for cp in window(slot, e, i):
                cp.wait()
        zero_scr[...] = jnp.zeros_like(zero_scr)
        tm = zero_scr.shape[0] // SUBLANES

        def zero_fill(row_start):
            return _window_copy(zero_scr, _rows_at(hs_ref, row_start, tm), sync_sem)

        for fills in ([zero_fill(start_ref[e] + count_ref[e]) for e in range(N_EXPERTS)],
                      [zero_fill(start_ref[e] + comp_ref[e]) for e in range(N_EXPERTS)]):
            for e, cp in enumerate(fills):
                cp.start(priority=e % 2)
            for cp in fills:
                cp.wait()
        used = start_ref[N_EXPERTS - 1] + comp_ref[N_EXPERTS - 1] + tm
        n_tail = (hs_ref.shape[0] // SUBLANES - used) // tm

        def tail_start(k, c):
            zero_fill(used + k * tm).start()
            return c

        def tail_wait(k, c):
            zero_fill(used + k * tm).wait()
            return c

        lax.fori_loop(0, n_tail, tail_start, 0)
        lax.fori_loop(0, n_tail, tail_wait, 0)


def _dispatch(scalars, h_bf, eid, rank, n_rows):
    n, d = h_bf.shape
    t = DISPATCH_STEPS * MOVE_ROWS
    w = DISPATCH_WINDOW
    assert n % t == 0 and d == SUBLANES * LANES and w % (2 * SUBLANES) == 0 and w <= MOE_ROWS
    grid_spec = pltpu.PrefetchScalarGridSpec(
        num_scalar_prefetch=5,
        grid=(n // t,),
        in_specs=[pl.BlockSpec((t, d), lambda i, *_: (i, 0)),
                  pl.BlockSpec((TOP_K, t), lambda i, *_: (0, i)),
                  pl.BlockSpec((TOP_K, t), lambda i, *_: (0, i))],
        out_specs=pl.BlockSpec(memory_space=pl.ANY),
        scratch_shapes=[pltpu.VMEM((2, N_EXPERTS, w * SUBLANES, LANES), F32),
                        pltpu.VMEM((w * SUBLANES, LANES), F32),
                        pltpu.VMEM((MOE_ROWS * SUBLANES, LANES), F32),
                        pltpu.SemaphoreType.DMA((2,)),
                        pltpu.SemaphoreType.DMA],
    )
    return pl.pallas_call(
        _dispatch_kernel,
        grid_spec=grid_spec,
        out_shape=jax.ShapeDtypeStruct((n_rows * SUBLANES, LANES), F32),
        compiler_params=_cparams(("arbitrary",)),
        name="moe_dispatch",
    )(*scalars, h_bf, eid, rank)


def _expert_kernel(has_partial, has_cast, te_ref, tv_ref, tb_ref, hs_ref, wg_ref, wu_ref, wd_ref,
                   *rest):
    del te_ref, tb_ref
    rest = list(rest)
    partial_ref = rest.pop(0) if has_partial else None
    cast_in = [rest.pop(0) for _ in range(3)] if has_cast else []
    y_ref, *cast_out = rest
    i = pl.program_id(0)
    tm = hs_ref.shape[0] // SUBLANES

    @pl.when(tv_ref[i] > 0)
    def _():
        _cast_step(cast_in, cast_out)
        h = _from_tiles(hs_ref, tm).astype(BF16)
        part = _swiglu_chunk(h, wg_ref[0], wu_ref[0], wd_ref[0])
        if has_partial:
            part = part + _from_tiles(partial_ref, tm)
        _to_tiles(y_ref, part)

    @pl.when(tv_ref[i] == 0)
    def _():
        _cast_step(cast_in, cast_out)
        y_ref[...] = jnp.zeros_like(y_ref)


def _experts(tile_expert, tile_valid, tile_block, hs, first_chunk, wg, wu, wd):
    d = wg.shape[1]
    f = wg.shape[2]
    tm = MOE_ROWS
    fc = FFN_COLS
    n_tiles = hs.shape[0] // (tm * SUBLANES)
    n_fc = f // fc
    tile_in = pl.BlockSpec((tm * SUBLANES, LANES), lambda i, te, tv, tb: (tb[i], 0))
    weights = tuple(first_chunk)
    y = None
    for c in range(n_fc):
        in_specs = [tile_in,
                    pl.BlockSpec((1, d, fc), lambda i, te, tv, tb: (te[i], 0, 0)),
                    pl.BlockSpec((1, d, fc), lambda i, te, tv, tb: (te[i], 0, 0)),
                    pl.BlockSpec((1, fc, d), lambda i, te, tv, tb: (te[i], 0, 0))]
        operands = [hs, *weights]
        if y is not None:
            in_specs.append(tile_in)
            operands.append(y)
        out_specs = [pl.BlockSpec((tm * SUBLANES, LANES), lambda i, te, tv, tb: (i, 0))]
        out_shape = [jax.ShapeDtypeStruct(hs.shape, F32)]
        if c + 1 < n_fc:
            chunks = _cast_chunks(n_tiles, d, fc)
            cast_in, cast_out = _cast_specs(c + 1, chunks, d, fc, lambda i, *_: i)
            in_specs += cast_in
            operands += [wg, wu, wd]
            out_specs += cast_out
            out_shape += _cast_shapes(d, fc)
        out = pl.pallas_call(
            functools.partial(_expert_kernel, y is not None, c + 1 < n_fc),
            grid_spec=pltpu.PrefetchScalarGridSpec(
                num_scalar_prefetch=3, grid=(n_tiles,), in_specs=in_specs, out_specs=out_specs),
            out_shape=out_shape,
            compiler_params=_cparams(("arbitrary",)),
            name="moe_experts",
        )(tile_expert, tile_valid, tile_block, *operands)
        y, weights = out[0], tuple(out[1:])
    return y


def _split(v):
    hi = v.astype(BF16)
    lo = (v - hi.astype(F32)).astype(BF16)
    return hi, lo


def _dot3(q, y):
    qh, ql = _split(q)
    yh, yl = _split(y)
    return (jnp.dot(qh, yh, preferred_element_type=F32)
            + jnp.dot(qh, yl, preferred_element_type=F32)
            + jnp.dot(ql, yh, preferred_element_type=F32))


def _combine_kernel(start_ref, comp_ref, base_ref, ctile_ref, eid_ref, rank_ref, gate_ref, x_ref,
                    mod_ref, ng_ref, y_ref, o_ref, win_scr, ovf_scr, f_scr, sems, sync_sem):
    i = pl.program_id(0)
    n_steps = pl.num_programs(0)
    slot = i % 2
    w = COMBINE_WINDOW
    t = x_ref.shape[0]
    e0 = eid_ref[:, 0:1]
    e1 = eid_ref[:, 1:2]
    r0 = rank_ref[:, 0:1]
    r1 = rank_ref[:, 1:2]
    g0 = gate_ref[:, 0:1]
    g1 = gate_ref[:, 1:2]
    col = lax.broadcasted_iota(I32, (t, w), 1)

    def first_row(step, e, second):
        want = base_ref[step * N_EXPERTS + e] + (w if second else 0)
        return start_ref[e] + jnp.minimum(want, comp_ref[e] - w)

    def window(step_slot, e, step):
        return _window_copy(_rows_at(y_ref, first_row(step, e, False), w),
                            win_scr.at[step_slot, e], sems.at[step_slot])

    @pl.when(i == 0)
    def _():
        for e in range(N_EXPERTS):
            window(slot, e, i).start(priority=e % 2)

    @pl.when(i + 1 < n_steps)
    def _():
        for e in range(N_EXPERTS):
            window(1 - slot, e, i + 1).start(priority=e % 2)

    for e in range(N_EXPERTS):
        window(slot, e, i).wait()

    def weights(e, second):
        mine0 = e0 == e
        mine1 = e1 == e
        rank = jnp.where(mine0, r0, jnp.where(mine1, r1, -1))
        gate = jnp.where(mine0, g0, jnp.where(mine1, g1, 0.0))
        local = rank - base_ref[i * N_EXPERTS + e]
        in_window = (local >= w) if second else jnp.logical_and(local >= 0, local < w)
        pos = rank + start_ref[e] - first_row(i, e, second)
        return jnp.where(jnp.logical_and(in_window, pos == col), gate, 0.0)

    f = jnp.zeros((t, x_ref.shape[1]), F32)
    for e in range(0, N_EXPERTS, 2):
        q = jnp.concatenate([weights(e, False), weights(e + 1, False)], axis=1)
        y = jnp.concatenate([_from_tiles(win_scr.at[slot, e], w),
                             _from_tiles(win_scr.at[slot, e + 1], w)], axis=0)
        f = f + _dot3(q, y)
    f_scr[...] = f

    for e in range(N_EXPERTS):
        @pl.when(ctile_ref[i * N_EXPERTS + e] > w)
        def _():
            cp = _window_copy(_rows_at(y_ref, first_row(i, e, True), w), ovf_scr, sync_sem)
            cp.start()
            cp.wait()
            f_scr[...] += _dot3(weights(e, True), _from_tiles(ovf_scr, w))

    g2 = mod_ref[0, 5:6, :]
    o_ref[...] = x_ref[...] + g2 * _rms(f_scr[...], ng_ref[3:4, :])


def _combine(scalars, eid_c, rank_c, gate_c, x2d, seq, mod, ng, y):
    n, d = x2d.shape
    t = MOVE_ROWS
    w = COMBINE_WINDOW
    assert seq % t == 0 and t <= 2 * w and w <= MOE_ROWS
    per_seq = seq // t
    grid_spec = pltpu.PrefetchScalarGridSpec(
        num_scalar_prefetch=4,
        grid=(n // t,),
        in_specs=[pl.BlockSpec((t, TOP_K), lambda i, *_: (i, 0)),
                  pl.BlockSpec((t, TOP_K), lambda i, *_: (i, 0)),
                  pl.BlockSpec((t, TOP_K), lambda i, *_: (i, 0)),
                  pl.BlockSpec((t, d), lambda i, *_: (i, 0)),
                  pl.BlockSpec((1, 6, d), lambda i, *_: (i // per_seq, 0, 0)),
                  pl.BlockSpec((4, d), lambda i, *_: (0, 0)),
                  pl.BlockSpec(memory_space=pl.ANY)],
        out_specs=pl.BlockSpec((t, d), lambda i, *_: (i, 0)),
        scratch_shapes=[pltpu.VMEM((2, N_EXPERTS, w * SUBLANES, LANES), F32),
                        pltpu.VMEM((w * SUBLANES, LANES), F32),
                        pltpu.VMEM((t, d), F32),
                        pltpu.SemaphoreType.DMA((2,)),
                        pltpu.SemaphoreType.DMA],
    )
    return pl.pallas_call(
        _combine_kernel,
        grid_spec=grid_spec,
        out_shape=jax.ShapeDtypeStruct((n, d), F32),
        compiler_params=_cparams(("arbitrary",)),
        name="moe_combine",
    )(*scalars, eid_c, rank_c, gate_c, x2d, mod, ng, y)


def _moe(x2d, seq, mod, ng, router_w, router_b, first_chunk, wg, wu, wd):
    n, d = x2d.shape
    tm = MOE_ROWS
    h_bf, eid, gate, rank, base, cnt = _route(x2d, seq, mod, ng, router_w, router_b)
    counts = cnt[:, 0]
    comp = jnp.maximum(((counts + tm - 1) // tm) * tm, tm)
    ends = jnp.cumsum(comp + tm)
    starts = ends - (comp + tm)
    n_tiles = (TOP_K * n) // tm + 2 * N_EXPERTS
    tile_start = jnp.arange(n_tiles, dtype=I32) * tm
    tile_expert = jnp.minimum(jnp.sum(tile_start[:, None] >= ends[None, :], axis=1),
                              N_EXPERTS - 1).astype(I32)
    tile_valid = jnp.logical_and(tile_start < (starts + comp)[tile_expert],
                                 tile_start < ends[-1]).astype(I32)
    base = base[:, :, 0]

    def runs(step_base):
        nxt = jnp.concatenate([step_base[1:], counts[None, :]], axis=0)
        return step_base.reshape(-1), (nxt - step_base).reshape(-1)

    d_base, d_run = runs(base[::DISPATCH_STEPS])
    c_base, c_run = runs(base)
    hs = _dispatch((starts, counts, comp, d_base, d_run), h_bf, eid, rank, n_tiles * tm)
    tile_block = lax.cummax(jnp.where(tile_valid > 0, jnp.arange(n_tiles, dtype=I32), 0))
    y = _experts(tile_expert, tile_valid, tile_block, hs, first_chunk, wg, wu, wd)
    return _combine((starts, comp, c_base, c_run), eid.T, rank.T, gate.T, x2d, seq, mod, ng, y)


def kernel(x, c, w_ada, b_ada, norm_gain, w_in, b_in, ln_v_gain, ln_v_bias, w_spatial, b_spatial, conv_w, conv_b, ln_conv_gain, ln_conv_bias, group_gain, w_out, ffn_w_gate, ffn_w_up, ffn_w_down, router_w, router_b, moe_w_gate, moe_w_up, moe_w_down):
    bsz, seq, d = x.shape
    depth = w_ada.shape[0]
    mod_all = _ada(c, w_ada, b_ada).reshape(depth, bsz, 6, d)
    for l in range(depth):
        mod = mod_all[l]
        ng = norm_gain[l]
        x = _mix(x, mod, ng, w_in[l], b_in[l], ln_v_gain[l], ln_v_bias[l], w_spatial[l],
                 b_spatial[l], conv_w[l], conv_b[l], ln_conv_gain[l], ln_conv_bias[l],
                 group_gain[l], w_out[l])
        x2d = x.reshape(bsz * seq, d)
        i = l // 2
        if l % 2 == 0:
            moe = (moe_w_gate[i], moe_w_up[i], moe_w_down[i]) if l + 1 < depth else None
            x2d, first_chunk = _ffn(x2d, seq, mod, ng, ffn_w_gate[i], ffn_w_up[i], ffn_w_down[i],
                                    moe)
        else:
            x2d = _moe(x2d, seq, mod, ng, router_w[i], router_b[i], first_chunk, moe_w_gate[i],
                       moe_w_up[i], moe_w_down[i])
        x = x2d.reshape(bsz, seq, d)
    return x
```

```python
import functools

import jax
import jax.numpy as jnp
from jax import lax
from jax.experimental import pallas as pl
from jax.experimental.pallas import tpu as pltpu

F32 = jnp.float32
BF16 = jnp.bfloat16
I32 = jnp.int32

EPS = 1e-6
CHUNK = 64
GMLP_BLOCK = 128
N_HEADS_A = 8
CONV_WIDTH = 31
N_EXPERTS = 8
TOP_K = 2

LANES = 128
SUBLANES = 8
CONV_HALO = 32
VMEM_LIMIT = 56 * 1024 * 1024

SEQ_TILE = 512
SUB_TILE = 256
FFN_ROWS = 512
FFN_COLS = 1792
MOE_ROWS = 512
MOVE_ROWS = 256
COMBINE_WINDOW = MOVE_ROWS // 2
DISPATCH_STEPS = 2
DISPATCH_WINDOW = 192


def _rms(x, g):
    return x * lax.rsqrt(jnp.mean(x * x, axis=-1, keepdims=True) + EPS) * g


def _layer_norm(x, g, b):
    mu = jnp.mean(x, axis=-1, keepdims=True)
    xc = x - mu
    return xc * lax.rsqrt(jnp.mean(xc * xc, axis=-1, keepdims=True) + EPS) * g + b


_SQRT_2_OVER_PI = 0.7978845608028654


def _sigmoid(x):
    return 0.5 + 0.5 * jnp.tanh(0.5 * x)


def _silu(x):
    return x * _sigmoid(x)


def _gelu(x):
    inner = x * (_SQRT_2_OVER_PI + (_SQRT_2_OVER_PI * 0.044715) * (x * x))
    half = 0.5 * x
    return half + half * jnp.tanh(inner)


def _cparams(sem, vmem=VMEM_LIMIT):
    return pltpu.CompilerParams(dimension_semantics=sem, vmem_limit_bytes=vmem)


def _ada_kernel(c_ref, w_ref, b_ref, o_ref):
    c_hi, c_lo = _split(jax.nn.silu(c_ref[...]))
    w = w_ref[0].astype(BF16)
    o_ref[0] = (jnp.dot(c_hi, w, preferred_element_type=F32)
                + jnp.dot(c_lo, w, preferred_element_type=F32) + b_ref[0])


def _ada(c, w_ada, b_ada):
    depth, d, n6 = w_ada.shape
    bsz = c.shape[0]
    nc = 1536
    return pl.pallas_call(
        _ada_kernel,
        grid=(depth, n6 // nc),
        in_specs=[pl.BlockSpec((bsz, d), lambda l, j: (0, 0)),
                  pl.BlockSpec((1, d, nc), lambda l, j: (l, 0, j)),
                  pl.BlockSpec((1, 1, nc), lambda l, j: (l, 0, j))],
        out_specs=pl.BlockSpec((1, bsz, nc), lambda l, j: (l, 0, j)),
        out_shape=jax.ShapeDtypeStruct((depth, bsz, n6), F32),
        compiler_params=_cparams(("arbitrary", "arbitrary")),
        name="ada_mod",
    )(c, w_ada, b_ada.reshape(depth, 1, n6))


def _mix_kernel(x_ref, mod_ref, ng_ref, w_in_ref, b_in_ref, lnv_g_ref, lnv_b_ref, ws_ref,
                bs_ref, cw_ref, cb_ref, lnc_g_ref, lnc_b_ref, gg_ref, w_out_ref,
                o_ref, wsp_scr, xg_scr, sh_scr, yc_scr):
    ts = x_ref.shape[1]
    d_a = lnv_g_ref.shape[1]
    d_b = lnc_g_ref.shape[1]
    b = pl.program_id(0)
    s = pl.program_id(1)

    @pl.when(jnp.logical_and(b == 0, s == 0))
    def _():
        t_chunk = lax.broadcasted_iota(I32, (GMLP_BLOCK, GMLP_BLOCK), 0) // CHUNK
        s_chunk = lax.broadcasted_iota(I32, (GMLP_BLOCK, GMLP_BLOCK), 1) // CHUNK
        allowed = t_chunk >= s_chunk
        for j in range(N_HEADS_A // 2):
            lo = jnp.where(allowed, ws_ref[2 * j], 0.0).astype(BF16)
            hi = jnp.where(allowed, ws_ref[2 * j + 1], 0.0).astype(BF16)
            wsp_scr[j] = jnp.concatenate([lo, hi], axis=1)

    @pl.when(s == 0)
    def _():
        xg_scr[0:CONV_HALO, :] = jnp.zeros((CONV_HALO, d_b), F32)

    sh1 = mod_ref[0, 0:1, :]
    sc1 = mod_ref[0, 1:2, :]
    g1 = mod_ref[0, 2:3, :]
    head_dim = d_a // N_HEADS_A
    lane = lax.broadcasted_iota(I32, (GMLP_BLOCK, LANES), 1)
    first_head = lane < head_dim
    zero = jnp.zeros((GMLP_BLOCK, LANES), BF16)
    first_tap = CONV_HALO - (CONV_WIDTH - 1)
    keep = CONV_HALO - SUBLANES
    rows = 64
    sub = min(SUB_TILE, ts)

    z_all = []
    for q in range(ts // sub):
        x = x_ref[0, q * sub:(q + 1) * sub, :]
        h = _rms(x, ng_ref[0:1, :]) * (1.0 + sc1) + sh1
        z_all.append(jnp.dot(h.astype(BF16), w_in_ref[...], preferred_element_type=F32)
                     + b_in_ref[...])

    def out_proj(lo, ycat):
        y = jnp.dot(ycat, w_out_ref[...], preferred_element_type=F32)
        o_ref[0, lo:lo + sub, :] = x_ref[0, lo:lo + sub, :] + g1 * _rms(y, ng_ref[1:2, :])

    pending = None
    for q in range(ts // sub):
        lo = q * sub
        z = z_all[q]
        ua = z[:, 0:d_a]
        va = z[:, d_a:2 * d_a]
        ab = z[:, 2 * d_a:2 * d_a + d_b]
        gb = z[:, 2 * d_a + d_b:]

        u = _gelu(ua)
        v = _layer_norm(_gelu(va), lnv_g_ref[...], lnv_b_ref[...]).astype(BF16)
        blocks = []
        for n in range(sub // GMLP_BLOCK):
            cols = []
            for j in range(d_a // LANES):
                vc = v[n * GMLP_BLOCK:(n + 1) * GMLP_BLOCK, j * LANES:(j + 1) * LANES]
                rhs = jnp.concatenate([jnp.where(first_head, vc, zero),
                                       jnp.where(first_head, zero, vc)], axis=0)
                cols.append(jnp.dot(wsp_scr[j], rhs, preferred_element_type=F32))
            blocks.append(jnp.concatenate(cols, axis=1) + bs_ref[...])
        ya = u * jnp.concatenate(blocks, axis=0)

        xg_scr[CONV_HALO + lo:CONV_HALO + lo + sub, :] = ab * _sigmoid(gb)
        if pending is not None:
            out_proj(*pending)
        new_lo = lo if q == 0 else lo + keep
        new_hi = lo + sub + keep
        for r in range(1, SUBLANES):
            sh_scr[r, new_lo:new_hi, :] = xg_scr[new_lo + r:new_hi + r, :]
        for rc in range(sub // rows):
            base = lo + rc * rows
            for lc in range(d_b // LANES):
                ls = slice(lc * LANES, (lc + 1) * LANES)
                acc = jnp.broadcast_to(cb_ref[:, ls], (rows, LANES))
                for k in range(CONV_WIDTH):
                    off = first_tap + k
                    r = off % SUBLANES
                    r0 = base + off - r
                    if r == 0:
                        win = xg_scr[r0:r0 + rows, ls]
                    else:
                        win = sh_scr[r, r0:r0 + rows, ls]
                    acc = acc + cw_ref[k:k + 1, ls] * win
                yc_scr[base:base + rows, ls] = acc
        yb = _silu(_layer_norm(yc_scr[lo:lo + sub, :], lnc_g_ref[...], lnc_b_ref[...]))

        ycat = jnp.concatenate([_rms(ya, gg_ref[:, 0:d_a]), _rms(yb, gg_ref[:, d_a:])], axis=1)
        pending = (lo, ycat.astype(BF16))

    out_proj(*pending)
    xg_scr[0:CONV_HALO, :] = xg_scr[ts:ts + CONV_HALO, :]


def _mix(x, mod, ng, w_in, b_in, lnv_g, lnv_b, w_sp, b_sp, cw, cb, lnc_g, lnc_b, gg, w_out):
    bsz, seq, d = x.shape
    d_in = w_in.shape[1]
    d_a = lnv_g.shape[0]
    d_b = lnc_g.shape[0]
    ts = min(SEQ_TILE, seq)
    assert seq % ts == 0 and ts % GMLP_BLOCK == 0 and ts >= CONV_HALO
    bs_full = jnp.repeat(b_sp.T, d_a // N_HEADS_A, axis=1)
    const = lambda *shape: pl.BlockSpec(shape, lambda b, s: (0,) * len(shape))
    assert ts % min(SUB_TILE, ts) == 0
    return pl.pallas_call(
        _mix_kernel,
        grid=(bsz, seq // ts),
        in_specs=[pl.BlockSpec((1, ts, d), lambda b, s: (b, s, 0)),
                  pl.BlockSpec((1, 6, d), lambda b, s: (b, 0, 0)),
                  const(4, d), const(d, d_in), const(1, d_in), const(1, d_a), const(1, d_a),
                  const(N_HEADS_A, GMLP_BLOCK, GMLP_BLOCK), const(GMLP_BLOCK, d_a),
                  const(CONV_WIDTH, d_b), const(1, d_b), const(1, d_b), const(1, d_b),
                  const(1, d_a + d_b), const(d_a + d_b, d)],
        out_specs=pl.BlockSpec((1, ts, d), lambda b, s: (b, s, 0)),
        out_shape=jax.ShapeDtypeStruct((bsz, seq, d), F32),
        scratch_shapes=[pltpu.VMEM((N_HEADS_A // 2, GMLP_BLOCK, 2 * GMLP_BLOCK), BF16),
                        pltpu.VMEM((ts + CONV_HALO, d_b), F32),
                        pltpu.VMEM((SUBLANES, ts + CONV_HALO, d_b), F32),
                        pltpu.VMEM((ts, d_b), F32)],
        compiler_params=_cparams(("arbitrary", "arbitrary")),
        name="token_mix",
    )(x, mod, ng, w_in.astype(BF16), b_in.reshape(1, d_in), lnv_g.reshape(1, d_a),
      lnv_b.reshape(1, d_a), w_sp, bs_full, cw, cb.reshape(1, d_b), lnc_g.reshape(1, d_b),
      lnc_b.reshape(1, d_b), gg.reshape(1, d_a + d_b), w_out.astype(BF16))


def _swiglu_chunk(h, wg, wu, wd):
    g = jnp.dot(h, wg, preferred_element_type=F32)
    u = jnp.dot(h, wu, preferred_element_type=F32)
    a = (_silu(g) * u).astype(BF16)
    return jnp.dot(a, wd, preferred_element_type=F32)


def _cast_chunks(steps, d, fc):
    for chunks in (8, 4, 2, 1):
        packed_rows = 2 * SUBLANES * chunks
        if N_EXPERTS * chunks <= steps and d % packed_rows == 0 and fc % packed_rows == 0:
            return chunks
    raise ValueError("too few grid steps to convert the expert weights")


def _cast_specs(chunk, chunks, d, fc, step_of):
    last = N_EXPERTS * chunks - 1

    def slab(*grid):
        s = jnp.minimum(step_of(*grid), last)
        return s // chunks, s % chunks

    up = (1, d // chunks, fc)
    down = (1, fc // chunks, d)
    ins = [pl.BlockSpec(up, lambda *g: slab(*g) + (chunk,)),
           pl.BlockSpec(up, lambda *g: slab(*g) + (chunk,)),
           pl.BlockSpec(down, lambda *g: (slab(*g)[0], chunk * chunks + slab(*g)[1], 0))]
    outs = [pl.BlockSpec(up, lambda *g: slab(*g) + (0,)),
            pl.BlockSpec(up, lambda *g: slab(*g) + (0,)),
            pl.BlockSpec(down, lambda *g: slab(*g) + (0,))]
    return ins, outs


def _cast_shapes(d, fc):
    return [jax.ShapeDtypeStruct((N_EXPERTS, d, fc), BF16),
            jax.ShapeDtypeStruct((N_EXPERTS, d, fc), BF16),
            jax.ShapeDtypeStruct((N_EXPERTS, fc, d), BF16)]


def _cast_step(srcs, dsts):
    for src, dst in zip(srcs, dsts):
        dst[...] = src[...].astype(BF16)


def _ffn_kernel(has_cast, x_ref, mod_ref, ng_ref, wg_ref, wu_ref, wd_ref, *rest):
    if has_cast:
        cast_in, (o_ref, *cast_out), (h_scr, acc_scr) = rest[0:3], rest[3:7], rest[7:9]
    else:
        o_ref, h_scr, acc_scr = rest
    j = pl.program_id(1)

    @pl.when(j == 0)
    def _():
        sh2 = mod_ref[0, 3:4, :]
        sc2 = mod_ref[0, 4:5, :]
        h = _rms(x_ref[...], ng_ref[2:3, :]) * (1.0 + sc2) + sh2
        h_scr[...] = h.astype(BF16)
        acc_scr[...] = jnp.zeros_like(acc_scr)

    if has_cast:
        _cast_step(cast_in, cast_out)
    acc_scr[...] += _swiglu_chunk(h_scr[...], wg_ref[...], wu_ref[...], wd_ref[...])

    @pl.when(j == pl.num_programs(1) - 1)
    def _():
        g2 = mod_ref[0, 5:6, :]
        o_ref[...] = x_ref[...] + g2 * _rms(acc_scr[...], ng_ref[3:4, :])


def _ffn(x2d, seq, mod, ng, wg, wu, wd, moe_weights=None):
    n, d = x2d.shape
    f = wg.shape[1]
    tm = min(FFN_ROWS, seq)
    fc = FFN_COLS
    assert seq % tm == 0 and f % fc == 0
    per_seq = seq // tm
    n_fc = f // fc
    in_specs = [pl.BlockSpec((tm, d), lambda i, j: (i, 0)),
                pl.BlockSpec((1, 6, d), lambda i, j: (i // per_seq, 0, 0)),
                pl.BlockSpec((4, d), lambda i, j: (0, 0)),
                pl.BlockSpec((d, fc), lambda i, j: (0, j)),
                pl.BlockSpec((d, fc), lambda i, j: (0, j)),
                pl.BlockSpec((fc, d), lambda i, j: (j, 0))]
    out_specs = [pl.BlockSpec((tm, d), lambda i, j: (i, 0))]
    out_shape = [jax.ShapeDtypeStruct((n, d), F32)]
    operands = [x2d, mod, ng, wg.astype(BF16), wu.astype(BF16), wd.astype(BF16)]
    if moe_weights is not None:
        chunks = _cast_chunks((n // tm) * n_fc, d, fc)
        cast_in, cast_out = _cast_specs(0, chunks, d, fc, lambda i, j: i * n_fc + j)
        in_specs += cast_in
        out_specs += cast_out
        out_shape += _cast_shapes(d, fc)
        operands += list(moe_weights)
    out = pl.pallas_call(
        functools.partial(_ffn_kernel, moe_weights is not None),
        grid=(n // tm, n_fc),
        in_specs=in_specs,
        out_specs=out_specs,
        out_shape=out_shape,
        scratch_shapes=[pltpu.VMEM((tm, d), BF16), pltpu.VMEM((tm, d), F32)],
        compiler_params=_cparams(("arbitrary", "arbitrary")),
        name="ffn_dense",
    )(*operands)
    return out[0], tuple(out[1:])


def _route_kernel(x_ref, mod_ref, ng_ref, rw_ref, rb_ref, h_ref, eid_ref, gate_ref, rank_ref,
                  base_ref, cnt_ref, run_scr):
    i = pl.program_id(0)
    tr = x_ref.shape[0]

    @pl.when(i == 0)
    def _():
        run_scr[...] = jnp.zeros_like(run_scr)

    base_ref[0] = run_scr[...].astype(I32)
    sh2 = mod_ref[0, 3:4, :]
    sc2 = mod_ref[0, 4:5, :]
    h = _rms(x_ref[...], ng_ref[2:3, :]) * (1.0 + sc2) + sh2
    h_hi, h_lo = _split(h)
    h_ref[...] = h_hi
    w_hi, w_lo = _split(rw_ref[...])
    contract_last = (((1,), (1,)), ((), ()))
    logits = (lax.dot_general(w_hi, h_hi, contract_last, preferred_element_type=F32)
              + lax.dot_general(w_hi, h_lo, contract_last, preferred_element_type=F32)
              + lax.dot_general(w_lo, h_hi, contract_last, preferred_element_type=F32)
              + rb_ref[...])
    e_iota = lax.broadcasted_iota(I32, logits.shape, 0)
    m1 = jnp.max(logits, axis=0, keepdims=True)
    i1 = jnp.min(jnp.where(logits == m1, e_iota, N_EXPERTS), axis=0, keepdims=True)
    oh1 = e_iota == i1
    rest = jnp.where(oh1, -jnp.inf, logits)
    m2 = jnp.max(rest, axis=0, keepdims=True)
    i2 = jnp.min(jnp.where(rest == m2, e_iota, N_EXPERTS), axis=0, keepdims=True)
    oh2 = e_iota == i2
    e2 = jnp.exp(m2 - m1)
    den = 1.0 + e2
    gate_ref[...] = jnp.concatenate([1.0 / den, e2 / den], axis=0)
    eid_ref[...] = jnp.concatenate([i1, i2], axis=0)

    member = oh1.astype(F32) + oh2.astype(F32)
    before = (lax.broadcasted_iota(I32, (tr, tr), 0) <
              lax.broadcasted_iota(I32, (tr, tr), 1)).astype(BF16)
    prefix = jnp.dot(member.astype(BF16), before, preferred_element_type=F32) + run_scr[:, 0:1]
    r1 = jnp.sum(jnp.where(oh1, prefix, 0.0), axis=0, keepdims=True)
    r2 = jnp.sum(jnp.where(oh2, prefix, 0.0), axis=0, keepdims=True)
    rank_ref[...] = jnp.concatenate([r1, r2], axis=0).astype(I32)
    run_scr[...] += jnp.sum(member, axis=1, keepdims=True)
    cnt_ref[...] = run_scr[...].astype(I32)


def _route(x2d, seq, mod, ng, router_w, router_b):
    n, d = x2d.shape
    tr = MOVE_ROWS
    assert seq % tr == 0
    per_seq = seq // tr
    return pl.pallas_call(
        _route_kernel,
        grid=(n // tr,),
        in_specs=[pl.BlockSpec((tr, d), lambda i: (i, 0)),
                  pl.BlockSpec((1, 6, d), lambda i: (i // per_seq, 0, 0)),
                  pl.BlockSpec((4, d), lambda i: (0, 0)),
                  pl.BlockSpec((N_EXPERTS, d), lambda i: (0, 0)),
                  pl.BlockSpec((N_EXPERTS, 1), lambda i: (0, 0))],
        out_specs=[pl.BlockSpec((tr, d), lambda i: (i, 0)),
                   pl.BlockSpec((TOP_K, tr), lambda i: (0, i)),
                   pl.BlockSpec((TOP_K, tr), lambda i: (0, i)),
                   pl.BlockSpec((TOP_K, tr), lambda i: (0, i)),
                   pl.BlockSpec((1, N_EXPERTS, LANES), lambda i: (i, 0, 0)),
                   pl.BlockSpec((N_EXPERTS, LANES), lambda i: (0, 0))],
        out_shape=[jax.ShapeDtypeStruct((n, d), BF16),
                   jax.ShapeDtypeStruct((TOP_K, n), I32),
                   jax.ShapeDtypeStruct((TOP_K, n), F32),
                   jax.ShapeDtypeStruct((TOP_K, n), I32),
                   jax.ShapeDtypeStruct((n // tr, N_EXPERTS, LANES), I32),
                   jax.ShapeDtypeStruct((N_EXPERTS, LANES), I32)],
        scratch_shapes=[pltpu.VMEM((N_EXPERTS, LANES), F32)],
        compiler_params=_cparams(("arbitrary",)),
        name="moe_route",
    )(x2d, mod, ng, router_w.T, router_b.reshape(N_EXPERTS, 1))


def _window_copy(src, dst, sem):
    return pltpu.make_async_copy(src, dst, sem)


def _rows_at(ref, row, n_rows):
    return ref.at[pl.ds(pl.multiple_of(row * SUBLANES, SUBLANES), n_rows * SUBLANES), :]


def _to_tiles(dst, val):
    rows = val.shape[0]
    for jj in range(val.shape[1] // LANES):
        dst[pl.ds(jj, rows, stride=SUBLANES), :] = val[:, jj * LANES:(jj + 1) * LANES]


def _from_tiles(src, rows):
    return jnp.concatenate([src[pl.ds(jj, rows, stride=SUBLANES), :] for jj in range(SUBLANES)],
                           axis=1)


def _dispatch_kernel(start_ref, count_ref, comp_ref, base_ref, ctile_ref, h_ref, eid_ref,
                     rank_ref, hs_ref, win_scr, ovf_scr, zero_scr, sems, sync_sem):
    i = pl.program_id(0)
    n_steps = pl.num_programs(0)
    slot = i % 2
    w = DISPATCH_WINDOW
    t = h_ref.shape[0]
    h = h_ref[...]
    e0 = eid_ref[0:1, :]
    e1 = eid_ref[1:2, :]
    r0 = rank_ref[0:1, :]
    r1 = rank_ref[1:2, :]
    row = lax.broadcasted_iota(I32, (w, t), 0)

    def local_rank(e):
        return jnp.where(e0 == e, r0, jnp.where(e1 == e, r1, -1)) - base_ref[i * N_EXPERTS + e]

    def selector(e, first_row):
        return jnp.where(row + first_row == local_rank(e), 1.0, 0.0).astype(BF16)

    slab = jnp.dot(jnp.concatenate([selector(e, 0) for e in range(N_EXPERTS)], axis=0), h,
                   preferred_element_type=F32)
    for e in range(N_EXPERTS):
        _to_tiles(win_scr.at[slot, e], slab[e * w:(e + 1) * w, :])

    def window(step_slot, e, step):
        dst_row = start_ref[e] + base_ref[step * N_EXPERTS + e]
        return _window_copy(win_scr.at[step_slot, e], _rows_at(hs_ref, dst_row, w),
                            sems.at[step_slot])

    @pl.when(i > 0)
    def _():
        for e in range(N_EXPERTS):
            window(1 - slot, e, i - 1).wait()

    for e in range(N_EXPERTS):
        window(slot, e, i).start()

    for e in range(N_EXPERTS):
        run = ctile_ref[i * N_EXPERTS + e]

        @pl.when(run > w)
        def _():
            def extra(k, c):
                first = k * w
                _to_tiles(ovf_scr, jnp.dot(selector(e, first), h, preferred_element_type=F32))
                dst_row = start_ref[e] + base_ref[i * N_EXPERTS + e] + first
                cp = _window_copy(ovf_scr, _rows_at(hs_ref, dst_row, w), sync_sem)
                cp.start()
                cp.wait()
                return c

            lax.fori_loop(1, (run + w - 1) // w, extra, 0)

    @pl.when(i == n_steps - 1)
    def _():
        for e in range(N_EXPERTS):
            window(slot, e, i).wait()
        zero_scr[...] = jnp.zeros_like(zero_scr)
        tm = zero_scr.shape[0] // SUBLANES

        def zero_fill(row_start):
            return _window_copy(zero_scr, _rows_at(hs_ref, row_start, tm), sync_sem)

        for fills in ([zero_fill(start_ref[e] + count_ref[e]) for e in range(N_EXPERTS)],
                      [zero_fill(start_ref[e] + comp_ref[e]) for e in range(N_EXPERTS)]):
            for cp in fills:
                cp.start()
            for cp in fills:
                cp.wait()
        used = start_ref[N_EXPERTS - 1] + comp_ref[N_EXPERTS - 1] + tm
        n_tail = (hs_ref.shape[0] // SUBLANES - used) // tm

        def tail_start(k, c):
            zero_fill(used + k * tm).start()
            return c

        def tail_wait(k, c):
            zero_fill(used + k * tm).wait()
            return c

        lax.fori_loop(0, n_tail, tail_start, 0)
        lax.fori_loop(0, n_tail, tail_wait, 0)


def _dispatch(scalars, h_bf, eid, rank, n_rows):
    n, d = h_bf.shape
    t = DISPATCH_STEPS * MOVE_ROWS
    w = DISPATCH_WINDOW
    assert n % t == 0 and d == SUBLANES * LANES and w % (2 * SUBLANES) == 0 and w <= MOE_ROWS
    grid_spec = pltpu.PrefetchScalarGridSpec(
        num_scalar_prefetch=5,
        grid=(n // t,),
        in_specs=[pl.BlockSpec((t, d), lambda i, *_: (i, 0)),
                  pl.BlockSpec((TOP_K, t), lambda i, *_: (0, i)),
                  pl.BlockSpec((TOP_K, t), lambda i, *_: (0, i))],
        out_specs=pl.BlockSpec(memory_space=pl.ANY),
        scratch_shapes=[pltpu.VMEM((2, N_EXPERTS, w * SUBLANES, LANES), F32),
                        pltpu.VMEM((w * SUBLANES, LANES), F32),
                        pltpu.VMEM((MOE_ROWS * SUBLANES, LANES), F32),
                        pltpu.SemaphoreType.DMA((2,)),
                        pltpu.SemaphoreType.DMA],
    )
    return pl.pallas_call(
        _dispatch_kernel,
        grid_spec=grid_spec,
        out_shape=jax.ShapeDtypeStruct((n_rows * SUBLANES, LANES), F32),
        compiler_params=_cparams(("arbitrary",)),
        name="moe_dispatch",
    )(*scalars, h_bf, eid, rank)


def _expert_kernel(has_partial, has_cast, te_ref, tv_ref, tb_ref, hs_ref, wg_ref, wu_ref, wd_ref,
                   *rest):
    del te_ref, tb_ref
    rest = list(rest)
    partial_ref = rest.pop(0) if has_partial else None
    cast_in = [rest.pop(0) for _ in range(3)] if has_cast else []
    y_ref, *cast_out = rest
    i = pl.program_id(0)
    tm = hs_ref.shape[0] // SUBLANES

    @pl.when(tv_ref[i] > 0)
    def _():
        _cast_step(cast_in, cast_out)
        h = _from_tiles(hs_ref, tm).astype(BF16)
        part = _swiglu_chunk(h, wg_ref[0], wu_ref[0], wd_ref[0])
        if has_partial:
            part = part + _from_tiles(partial_ref, tm)
        _to_tiles(y_ref, part)

    @pl.when(tv_ref[i] == 0)
    def _():
        _cast_step(cast_in, cast_out)
        y_ref[...] = jnp.zeros_like(y_ref)


def _experts(tile_expert, tile_valid, tile_block, hs, first_chunk, wg, wu, wd):
    d = wg.shape[1]
    f = wg.shape[2]
    tm = MOE_ROWS
    fc = FFN_COLS
    n_tiles = hs.shape[0] // (tm * SUBLANES)
    n_fc = f // fc
    tile_in = pl.BlockSpec((tm * SUBLANES, LANES), lambda i, te, tv, tb: (tb[i], 0))
    weights = tuple(first_chunk)
    y = None
    for c in range(n_fc):
        in_specs = [tile_in,
                    pl.BlockSpec((1, d, fc), lambda i, te, tv, tb: (te[i], 0, 0)),
                    pl.BlockSpec((1, d, fc), lambda i, te, tv, tb: (te[i], 0, 0)),
                    pl.BlockSpec((1, fc, d), lambda i, te, tv, tb: (te[i], 0, 0))]
        operands = [hs, *weights]
        if y is not None:
            in_specs.append(tile_in)
            operands.append(y)
        out_specs = [pl.BlockSpec((tm * SUBLANES, LANES), lambda i, te, tv, tb: (i, 0))]
        out_shape = [jax.ShapeDtypeStruct(hs.shape, F32)]
        if c + 1 < n_fc:
            chunks = _cast_chunks(n_tiles, d, fc)
            cast_in, cast_out = _cast_specs(c + 1, chunks, d, fc, lambda i, *_: i)
            in_specs += cast_in
            operands += [wg, wu, wd]
            out_specs += cast_out
            out_shape += _cast_shapes(d, fc)
        out = pl.pallas_call(
            functools.partial(_expert_kernel, y is not None, c + 1 < n_fc),
            grid_spec=pltpu.PrefetchScalarGridSpec(
                num_scalar_prefetch=3, grid=(n_tiles,), in_specs=in_specs, out_specs=out_specs),
            out_shape=out_shape,
            compiler_params=_cparams(("arbitrary",)),
            name="moe_experts",
        )(tile_expert, tile_valid, tile_block, *operands)
        y, weights = out[0], tuple(out[1:])
    return y


def _split(v):
    hi = v.astype(BF16)
    lo = (v - hi.astype(F32)).astype(BF16)
    return hi, lo


def _dot3(q, y):
    qh, ql = _split(q)
    yh, yl = _split(y)
    return (jnp.dot(qh, yh, preferred_element_type=F32)
            + jnp.dot(qh, yl, preferred_element_type=F32)
            + jnp.dot(ql, yh, preferred_element_type=F32))


def _combine_kernel(start_ref, comp_ref, base_ref, ctile_ref, eid_ref, rank_ref, gate_ref, x_ref,
                    mod_ref, ng_ref, y_ref, o_ref, win_scr, ovf_scr, f_scr, sems, sync_sem):
    i = pl.program_id(0)
    n_steps = pl.num_programs(0)
    slot = i % 2
    w = COMBINE_WINDOW
    t = x_ref.shape[0]
    e0 = eid_ref[:, 0:1]
    e1 = eid_ref[:, 1:2]
    r0 = rank_ref[:, 0:1]
    r1 = rank_ref[:, 1:2]
    g0 = gate_ref[:, 0:1]
    g1 = gate_ref[:, 1:2]
    col = lax.broadcasted_iota(I32, (t, w), 1)

    def first_row(step, e, second):
        want = base_ref[step * N_EXPERTS + e] + (w if second else 0)
        return start_ref[e] + jnp.minimum(want, comp_ref[e] - w)

    def window(step_slot, e, step):
        return _window_copy(_rows_at(y_ref, first_row(step, e, False), w),
                            win_scr.at[step_slot, e], sems.at[step_slot])

    @pl.when(i == 0)
    def _():
        for e in range(N_EXPERTS):
            window(slot, e, i).start()

    @pl.when(i + 1 < n_steps)
    def _():
        for e in range(N_EXPERTS):
            window(1 - slot, e, i + 1).start()

    for e in range(N_EXPERTS):
        window(slot, e, i).wait()

    def weights(e, second):
        mine0 = e0 == e
        mine1 = e1 == e
        rank = jnp.where(mine0, r0, jnp.where(mine1, r1, -1))
        gate = jnp.where(mine0, g0, jnp.where(mine1, g1, 0.0))
        local = rank - base_ref[i * N_EXPERTS + e]
        in_window = (local >= w) if second else jnp.logical_and(local >= 0, local < w)
        pos = rank + start_ref[e] - first_row(i, e, second)
        return jnp.where(jnp.logical_and(in_window, pos == col), gate, 0.0)

    f = jnp.zeros((t, x_ref.shape[1]), F32)
    for e in range(0, N_EXPERTS, 2):
        q = jnp.concatenate([weights(e, False), weights(e + 1, False)], axis=1)
        y = jnp.concatenate([_from_tiles(win_scr.at[slot, e], w),
                             _from_tiles(win_scr.at[slot, e + 1], w)], axis=0)
        f = f + _dot3(q, y)
    f_scr[...] = f

    for e in range(N_EXPERTS):
        @pl.when(ctile_ref[i * N_EXPERTS + e] > w)
        def _():
            cp = _window_copy(_rows_at(y_ref, first_row(i, e, True), w), ovf_scr, sync_sem)
            cp.start()
            cp.wait()
            f_scr[...] += _dot3(weights(e, True), _from_tiles(ovf_scr, w))

    g2 = mod_ref[0, 5:6, :]
    o_ref[...] = x_ref[...] + g2 * _rms(f_scr[...], ng_ref[3:4, :])


def _combine(scalars, eid_c, rank_c, gate_c, x2d, seq, mod, ng, y):
    n, d = x2d.shape
    t = MOVE_ROWS
    w = COMBINE_WINDOW
    assert seq % t == 0 and t <= 2 * w and w <= MOE_ROWS
    per_seq = seq // t
    grid_spec = pltpu.PrefetchScalarGridSpec(
        num_scalar_prefetch=4,
        grid=(n // t,),
        in_specs=[pl.BlockSpec((t, TOP_K), lambda i, *_: (i, 0)),
                  pl.BlockSpec((t, TOP_K), lambda i, *_: (i, 0)),
                  pl.BlockSpec((t, TOP_K), lambda i, *_: (i, 0)),
                  pl.BlockSpec((t, d), lambda i, *_: (i, 0)),
                  pl.BlockSpec((1, 6, d), lambda i, *_: (i // per_seq, 0, 0)),
                  pl.BlockSpec((4, d), lambda i, *_: (0, 0)),
                  pl.BlockSpec(memory_space=pl.ANY)],
        out_specs=pl.BlockSpec((t, d), lambda i, *_: (i, 0)),
        scratch_shapes=[pltpu.VMEM((2, N_EXPERTS, w * SUBLANES, LANES), F32),
                        pltpu.VMEM((w * SUBLANES, LANES), F32),
                        pltpu.VMEM((t, d), F32),
                        pltpu.SemaphoreType.DMA((2,)),
                        pltpu.SemaphoreType.DMA],
    )
    return pl.pallas_call(
        _combine_kernel,
        grid_spec=grid_spec,
        out_shape=jax.ShapeDtypeStruct((n, d), F32),
        compiler_params=_cparams(("arbitrary",)),
        name="moe_combine",
    )(*scalars, eid_c, rank_c, gate_c, x2d, mod, ng, y)


def _moe(x2d, seq, mod, ng, router_w, router_b, first_chunk, wg, wu, wd):
    n, d = x2d.shape
    tm = MOE_ROWS
    h_bf, eid, gate, rank, base, cnt = _route(x2d, seq, mod, ng, router_w, router_b)
    counts = cnt[:, 0]
    comp = jnp.maximum(((counts + tm - 1) // tm) * tm, tm)
    ends = jnp.cumsum(comp + tm)
    starts = ends - (comp + tm)
    n_tiles = (TOP_K * n) // tm + 2 * N_EXPERTS
    tile_start = jnp.arange(n_tiles, dtype=I32) * tm
    tile_expert = jnp.minimum(jnp.sum(tile_start[:, None] >= ends[None, :], axis=1),
                              N_EXPERTS - 1).astype(I32)
    tile_valid = jnp.logical_and(tile_start < (starts + comp)[tile_expert],
                                 tile_start < ends[-1]).astype(I32)
    base = base[:, :, 0]

    def runs(step_base):
        nxt = jnp.concatenate([step_base[1:], counts[None, :]], axis=0)
        return step_base.reshape(-1), (nxt - step_base).reshape(-1)

    d_base, d_run = runs(base[::DISPATCH_STEPS])
    c_base, c_run = runs(base)
    hs = _dispatch((starts, counts, comp, d_base, d_run), h_bf, eid, rank, n_tiles * tm)
    tile_block = lax.cummax(jnp.where(tile_valid > 0, jnp.arange(n_tiles, dtype=I32), 0))
    y = _experts(tile_expert, tile_valid, tile_block, hs, first_chunk, wg, wu, wd)
    return _combine((starts, comp, c_base, c_run), eid.T, rank.T, gate.T, x2d, seq, mod, ng, y)


def kernel(x, c, w_ada, b_ada, norm_gain, w_in, b_in, ln_v_gain, ln_v_bias, w_spatial, b_spatial, conv_w, conv_b, ln_conv_gain, ln_conv_bias, group_gain, w_out, ffn_w_gate, ffn_w_up, ffn_w_down, router_w, router_b, moe_w_gate, moe_w_up, moe_w_down):
    bsz, seq, d = x.shape
    depth = w_ada.shape[0]
    mod_all = _ada(c, w_ada, b_ada).reshape(depth, bsz, 6, d)
    for l in range(depth):
        mod = mod_all[l]
        ng = norm_gain[l]
        x = _mix(x, mod, ng, w_in[l], b_in[l], ln_v_gain[l], ln_v_bias[l], w_spatial[l],
                 b_spatial[l], conv_w[l], conv_b[l], ln_conv_gain[l], ln_conv_bias[l],
                 group_gain[l], w_out[l])
        x2d = x.reshape(bsz * seq, d)
        i = l // 2
        if l % 2 == 0:
            moe = (moe_w_gate[i], moe_w_up[i], moe_w_down[i]) if l + 1 < depth else None
            x2d, first_chunk = _ffn(x2d, seq, mod, ng, ffn_w_gate[i], ffn_w_up[i], ffn_w_down[i],
                                    moe)
        else:
            x2d = _moe(x2d, seq, mod, ng, router_w[i], router_b[i], first_chunk, moe_w_gate[i],
                       moe_w_up[i], moe_w_down[i])
        x = x2d.reshape(bsz, seq, d)
    return x
```

```python
import functools

import jax
import jax.numpy as jnp
from jax import lax
from jax.experimental import pallas as pl
from jax.experimental.pallas import tpu as pltpu

F32 = jnp.float32
BF16 = jnp.bfloat16
I32 = jnp.int32

EPS = 1e-6
CHUNK = 64
GMLP_BLOCK = 128
N_HEADS_A = 8
CONV_WIDTH = 31
N_EXPERTS = 8
TOP_K = 2

LANES = 128
SUBLANES = 8
CONV_HALO = 32
VMEM_LIMIT = 56 * 1024 * 1024

SEQ_TILE = 512
SUB_TILE = 256
FFN_ROWS = 512
FFN_COLS = 1792
MOE_ROWS = 512
MOVE_ROWS = 256
COMBINE_WINDOW = MOVE_ROWS // 2
DISPATCH_STEPS = 2
DISPATCH_WINDOW = 192


def _rms(x, g):
    return x * lax.rsqrt(jnp.mean(x * x, axis=-1, keepdims=True) + EPS) * g


def _layer_norm(x, g, b):
    mu = jnp.mean(x, axis=-1, keepdims=True)
    xc = x - mu
    return xc * lax.rsqrt(jnp.mean(xc * xc, axis=-1, keepdims=True) + EPS) * g + b


_SQRT_2_OVER_PI = 0.7978845608028654


def _sigmoid(x):
    return 0.5 + 0.5 * jnp.tanh(0.5 * x)


def _silu(x):
    half = 0.5 * x
    return half + half * jnp.tanh(half)


def _gelu(x):
    inner = x * (_SQRT_2_OVER_PI + (_SQRT_2_OVER_PI * 0.044715) * (x * x))
    half = 0.5 * x
    return half + half * jnp.tanh(inner)


def _cparams(sem, vmem=VMEM_LIMIT):
    return pltpu.CompilerParams(dimension_semantics=sem, vmem_limit_bytes=vmem)


def _ada_kernel(c_ref, w_ref, b_ref, o_ref):
    c_hi, c_lo = _split(jax.nn.silu(c_ref[...]))
    w = w_ref[0].astype(BF16)
    o_ref[0] = (jnp.dot(c_hi, w, preferred_element_type=F32)
                + jnp.dot(c_lo, w, preferred_element_type=F32) + b_ref[0])


def _ada(c, w_ada, b_ada):
    depth, d, n6 = w_ada.shape
    bsz = c.shape[0]
    nc = 1536
    return pl.pallas_call(
        _ada_kernel,
        grid=(depth, n6 // nc),
        in_specs=[pl.BlockSpec((bsz, d), lambda l, j: (0, 0)),
                  pl.BlockSpec((1, d, nc), lambda l, j: (l, 0, j)),
                  pl.BlockSpec((1, 1, nc), lambda l, j: (l, 0, j))],
        out_specs=pl.BlockSpec((1, bsz, nc), lambda l, j: (l, 0, j)),
        out_shape=jax.ShapeDtypeStruct((depth, bsz, n6), F32),
        compiler_params=_cparams(("arbitrary", "arbitrary")),
        name="ada_mod",
    )(c, w_ada, b_ada.reshape(depth, 1, n6))


def _mix_kernel(x_ref, mod_ref, ng_ref, w_in_ref, b_in_ref, lnv_g_ref, lnv_b_ref, ws_ref,
                bs_ref, cw_ref, cb_ref, lnc_g_ref, lnc_b_ref, gg_ref, w_out_ref,
                o_ref, wsp_scr, xg_scr, sh_scr, yc_scr):
    ts = x_ref.shape[1]
    d_a = lnv_g_ref.shape[1]
    d_b = lnc_g_ref.shape[1]
    b = pl.program_id(0)
    s = pl.program_id(1)

    @pl.when(jnp.logical_and(b == 0, s == 0))
    def _():
        t_chunk = lax.broadcasted_iota(I32, (GMLP_BLOCK, GMLP_BLOCK), 0) // CHUNK
        s_chunk = lax.broadcasted_iota(I32, (GMLP_BLOCK, GMLP_BLOCK), 1) // CHUNK
        allowed = t_chunk >= s_chunk
        for j in range(N_HEADS_A // 2):
            lo = jnp.where(allowed, ws_ref[2 * j], 0.0).astype(BF16)
            hi = jnp.where(allowed, ws_ref[2 * j + 1], 0.0).astype(BF16)
            wsp_scr[j] = jnp.concatenate([lo, hi], axis=1)

    @pl.when(s == 0)
    def _():
        xg_scr[0:CONV_HALO, :] = jnp.zeros((CONV_HALO, d_b), F32)

    sh1 = mod_ref[0, 0:1, :]
    in_scale = ng_ref[0:1, :] * (1.0 + mod_ref[0, 1:2, :])
    out_scale = mod_ref[0, 2:3, :] * ng_ref[1:2, :]
    head_dim = d_a // N_HEADS_A
    lane = lax.broadcasted_iota(I32, (GMLP_BLOCK, LANES), 1)
    first_head = lane < head_dim
    zero = jnp.zeros((GMLP_BLOCK, LANES), BF16)
    first_tap = CONV_HALO - (CONV_WIDTH - 1)
    keep = CONV_HALO - SUBLANES
    rows = 64
    sub = min(SUB_TILE, ts)

    z_all = []
    for q in range(ts // sub):
        x = x_ref[0, q * sub:(q + 1) * sub, :]
        h = _rms(x, in_scale) + sh1
        z_all.append(jnp.dot(h.astype(BF16), w_in_ref[...], preferred_element_type=F32)
                     + b_in_ref[...])

    def out_proj(lo, ycat):
        y = jnp.dot(ycat, w_out_ref[...], preferred_element_type=F32)
        o_ref[0, lo:lo + sub, :] = x_ref[0, lo:lo + sub, :] + _rms(y, out_scale)

    pending = None
    for q in range(ts // sub):
        lo = q * sub
        z = z_all[q]
        ua = z[:, 0:d_a]
        va = z[:, d_a:2 * d_a]
        ab = z[:, 2 * d_a:2 * d_a + d_b]
        gb = z[:, 2 * d_a + d_b:]

        u = _gelu(ua)
        v = _layer_norm(_gelu(va), lnv_g_ref[...], lnv_b_ref[...]).astype(BF16)
        blocks = []
        for n in range(sub // GMLP_BLOCK):
            cols = []
            for j in range(d_a // LANES):
                vc = v[n * GMLP_BLOCK:(n + 1) * GMLP_BLOCK, j * LANES:(j + 1) * LANES]
                rhs = jnp.concatenate([jnp.where(first_head, vc, zero),
                                       jnp.where(first_head, zero, vc)], axis=0)
                cols.append(jnp.dot(wsp_scr[j], rhs, preferred_element_type=F32))
            blocks.append(jnp.concatenate(cols, axis=1) + bs_ref[...])
        ya = u * jnp.concatenate(blocks, axis=0)

        xg_scr[CONV_HALO + lo:CONV_HALO + lo + sub, :] = ab * _sigmoid(gb)
        if pending is not None:
            out_proj(*pending)
        new_lo = lo if q == 0 else lo + keep
        new_hi = lo + sub + keep
        for r in range(1, SUBLANES):
            sh_scr[r, new_lo:new_hi, :] = xg_scr[new_lo + r:new_hi + r, :]
        for rc in range(sub // rows):
            base = lo + rc * rows
            for lc in range(d_b // LANES):
                ls = slice(lc * LANES, (lc + 1) * LANES)
                acc = jnp.broadcast_to(cb_ref[:, ls], (rows, LANES))
                for k in range(CONV_WIDTH):
                    off = first_tap + k
                    r = off % SUBLANES
                    r0 = base + off - r
                    if r == 0:
                        win = xg_scr[r0:r0 + rows, ls]
                    else:
                        win = sh_scr[r, r0:r0 + rows, ls]
                    acc = acc + cw_ref[k:k + 1, ls] * win
                yc_scr[base:base + rows, ls] = acc
        yb = _silu(_layer_norm(yc_scr[lo:lo + sub, :], lnc_g_ref[...], lnc_b_ref[...]))

        ycat = jnp.concatenate([_rms(ya, gg_ref[:, 0:d_a]), _rms(yb, gg_ref[:, d_a:])], axis=1)
        pending = (lo, ycat.astype(BF16))

    out_proj(*pending)
    xg_scr[0:CONV_HALO, :] = xg_scr[ts:ts + CONV_HALO, :]


def _mix(x, mod, ng, w_in, b_in, lnv_g, lnv_b, w_sp, b_sp, cw, cb, lnc_g, lnc_b, gg, w_out):
    bsz, seq, d = x.shape
    d_in = w_in.shape[1]
    d_a = lnv_g.shape[0]
    d_b = lnc_g.shape[0]
    ts = min(SEQ_TILE, seq)
    assert seq % ts == 0 and ts % GMLP_BLOCK == 0 and ts >= CONV_HALO
    bs_full = jnp.repeat(b_sp.T, d_a // N_HEADS_A, axis=1)
    const = lambda *shape: pl.BlockSpec(shape, lambda b, s: (0,) * len(shape))
    assert ts % min(SUB_TILE, ts) == 0
    return pl.pallas_call(
        _mix_kernel,
        grid=(bsz, seq // ts),
        in_specs=[pl.BlockSpec((1, ts, d), lambda b, s: (b, s, 0)),
                  pl.BlockSpec((1, 6, d), lambda b, s: (b, 0, 0)),
                  const(4, d), const(d, d_in), const(1, d_in), const(1, d_a), const(1, d_a),
                  const(N_HEADS_A, GMLP_BLOCK, GMLP_BLOCK), const(GMLP_BLOCK, d_a),
                  const(CONV_WIDTH, d_b), const(1, d_b), const(1, d_b), const(1, d_b),
                  const(1, d_a + d_b), const(d_a + d_b, d)],
        out_specs=pl.BlockSpec((1, ts, d), lambda b, s: (b, s, 0)),
        out_shape=jax.ShapeDtypeStruct((bsz, seq, d), F32),
        scratch_shapes=[pltpu.VMEM((N_HEADS_A // 2, GMLP_BLOCK, 2 * GMLP_BLOCK), BF16),
                        pltpu.VMEM((ts + CONV_HALO, d_b), F32),
                        pltpu.VMEM((SUBLANES, ts + CONV_HALO, d_b), F32),
                        pltpu.VMEM((ts, d_b), F32)],
        compiler_params=_cparams(("arbitrary", "arbitrary")),
        name="token_mix",
    )(x, mod, ng, w_in.astype(BF16), b_in.reshape(1, d_in), lnv_g.reshape(1, d_a),
      lnv_b.reshape(1, d_a), w_sp, bs_full, cw, cb.reshape(1, d_b), lnc_g.reshape(1, d_b),
      lnc_b.reshape(1, d_b), gg.reshape(1, d_a + d_b), w_out.astype(BF16))


def _swiglu_chunk(h, wg, wu, wd):
    g = jnp.dot(h, wg, preferred_element_type=F32)
    u = jnp.dot(h, wu, preferred_element_type=F32)
    a = (_silu(g) * u).astype(BF16)
    return jnp.dot(a, wd, preferred_element_type=F32)


def _cast_chunks(steps, d, fc):
    for chunks in (8, 4, 2, 1):
        packed_rows = 2 * SUBLANES * chunks
        if N_EXPERTS * chunks <= steps and d % packed_rows == 0 and fc % packed_rows == 0:
            return chunks
    raise ValueError("too few grid steps to convert the expert weights")


def _cast_specs(chunk, chunks, d, fc, step_of):
    last = N_EXPERTS * chunks - 1

    def slab(*grid):
        s = jnp.minimum(step_of(*grid), last)
        return s // chunks, s % chunks

    up = (1, d // chunks, fc)
    down = (1, fc // chunks, d)
    ins = [pl.BlockSpec(up, lambda *g: slab(*g) + (chunk,)),
           pl.BlockSpec(up, lambda *g: slab(*g) + (chunk,)),
           pl.BlockSpec(down, lambda *g: (slab(*g)[0], chunk * chunks + slab(*g)[1], 0))]
    outs = [pl.BlockSpec(up, lambda *g: slab(*g) + (0,)),
            pl.BlockSpec(up, lambda *g: slab(*g) + (0,)),
            pl.BlockSpec(down, lambda *g: slab(*g) + (0,))]
    return ins, outs


def _cast_shapes(d, fc):
    return [jax.ShapeDtypeStruct((N_EXPERTS, d, fc), BF16),
            jax.ShapeDtypeStruct((N_EXPERTS, d, fc), BF16),
            jax.ShapeDtypeStruct((N_EXPERTS, fc, d), BF16)]


def _cast_step(srcs, dsts):
    for src, dst in zip(srcs, dsts):
        dst[...] = src[...].astype(BF16)


def _ffn_kernel(has_cast, x_ref, mod_ref, ng_ref, wg_ref, wu_ref, wd_ref, *rest):
    if has_cast:
        cast_in, (o_ref, *cast_out), (h_scr, acc_scr) = rest[0:3], rest[3:7], rest[7:9]
    else:
        o_ref, h_scr, acc_scr = rest
    j = pl.program_id(1)

    @pl.when(j == 0)
    def _():
        sh2 = mod_ref[0, 3:4, :]
        sc2 = mod_ref[0, 4:5, :]
        h = _rms(x_ref[...], ng_ref[2:3, :]) * (1.0 + sc2) + sh2
        h_scr[...] = h.astype(BF16)
        acc_scr[...] = jnp.zeros_like(acc_scr)

    if has_cast:
        _cast_step(cast_in, cast_out)
    acc_scr[...] += _swiglu_chunk(h_scr[...], wg_ref[...], wu_ref[...], wd_ref[...])

    @pl.when(j == pl.num_programs(1) - 1)
    def _():
        g2 = mod_ref[0, 5:6, :]
        o_ref[...] = x_ref[...] + g2 * _rms(acc_scr[...], ng_ref[3:4, :])


def _ffn(x2d, seq, mod, ng, wg, wu, wd, moe_weights=None):
    n, d = x2d.shape
    f = wg.shape[1]
    tm = min(FFN_ROWS, seq)
    fc = FFN_COLS
    assert seq % tm == 0 and f % fc == 0
    per_seq = seq // tm
    n_fc = f // fc
    in_specs = [pl.BlockSpec((tm, d), lambda i, j: (i, 0)),
                pl.BlockSpec((1, 6, d), lambda i, j: (i // per_seq, 0, 0)),
                pl.BlockSpec((4, d), lambda i, j: (0, 0)),
                pl.BlockSpec((d, fc), lambda i, j: (0, j)),
                pl.BlockSpec((d, fc), lambda i, j: (0, j)),
                pl.BlockSpec((fc, d), lambda i, j: (j, 0))]
    out_specs = [pl.BlockSpec((tm, d), lambda i, j: (i, 0))]
    out_shape = [jax.ShapeDtypeStruct((n, d), F32)]
    operands = [x2d, mod, ng, wg.astype(BF16), wu.astype(BF16), wd.astype(BF16)]
    if moe_weights is not None:
        chunks = _cast_chunks((n // tm) * n_fc, d, fc)
        cast_in, cast_out = _cast_specs(0, chunks, d, fc, lambda i, j: i * n_fc + j)
        in_specs += cast_in
        out_specs += cast_out
        out_shape += _cast_shapes(d, fc)
        operands += list(moe_weights)
    out = pl.pallas_call(
        functools.partial(_ffn_kernel, moe_weights is not None),
        grid=(n // tm, n_fc),
        in_specs=in_specs,
        out_specs=out_specs,
        out_shape=out_shape,
        scratch_shapes=[pltpu.VMEM((tm, d), BF16), pltpu.VMEM((tm, d), F32)],
        compiler_params=_cparams(("arbitrary", "arbitrary")),
        name="ffn_dense",
    )(*operands)
    return out[0], tuple(out[1:])


def _route_kernel(x_ref, mod_ref, ng_ref, rw_ref, rb_ref, h_ref, eid_ref, gate_ref, rank_ref,
                  base_ref, cnt_ref, run_scr):
    i = pl.program_id(0)
    tr = x_ref.shape[0]

    @pl.when(i == 0)
    def _():
        run_scr[...] = jnp.zeros_like(run_scr)

    base_ref[0] = run_scr[...].astype(I32)
    sh2 = mod_ref[0, 3:4, :]
    sc2 = mod_ref[0, 4:5, :]
    h = _rms(x_ref[...], ng_ref[2:3, :]) * (1.0 + sc2) + sh2
    h_hi, h_lo = _split(h)
    h_ref[...] = h_hi
    w_hi, w_lo = _split(rw_ref[...])
    contract_last = (((1,), (1,)), ((), ()))
    logits = (lax.dot_general(w_hi, h_hi, contract_last, preferred_element_type=F32)
              + lax.dot_general(w_hi, h_lo, contract_last, preferred_element_type=F32)
              + lax.dot_general(w_lo, h_hi, contract_last, preferred_element_type=F32)
              + rb_ref[...])
    e_iota = lax.broadcasted_iota(I32, logits.shape, 0)
    m1 = jnp.max(logits, axis=0, keepdims=True)
    i1 = jnp.min(jnp.where(logits == m1, e_iota, N_EXPERTS), axis=0, keepdims=True)
    oh1 = e_iota == i1
    rest = jnp.where(oh1, -jnp.inf, logits)
    m2 = jnp.max(rest, axis=0, keepdims=True)
    i2 = jnp.min(jnp.where(rest == m2, e_iota, N_EXPERTS), axis=0, keepdims=True)
    oh2 = e_iota == i2
    e2 = jnp.exp(m2 - m1)
    den = 1.0 + e2
    gate_ref[...] = jnp.concatenate([1.0 / den, e2 / den], axis=0)
    eid_ref[...] = jnp.concatenate([i1, i2], axis=0)

    member = oh1.astype(F32) + oh2.astype(F32)
    before = (lax.broadcasted_iota(I32, (tr, tr), 0) <
              lax.broadcasted_iota(I32, (tr, tr), 1)).astype(BF16)
    prefix = jnp.dot(member.astype(BF16), before, preferred_element_type=F32) + run_scr[:, 0:1]
    r1 = jnp.sum(jnp.where(oh1, prefix, 0.0), axis=0, keepdims=True)
    r2 = jnp.sum(jnp.where(oh2, prefix, 0.0), axis=0, keepdims=True)
    rank_ref[...] = jnp.concatenate([r1, r2], axis=0).astype(I32)
    run_scr[...] += jnp.sum(member, axis=1, keepdims=True)
    cnt_ref[...] = run_scr[...].astype(I32)


def _route(x2d, seq, mod, ng, router_w, router_b):
    n, d = x2d.shape
    tr = MOVE_ROWS
    assert seq % tr == 0
    per_seq = seq // tr
    return pl.pallas_call(
        _route_kernel,
        grid=(n // tr,),
        in_specs=[pl.BlockSpec((tr, d), lambda i: (i, 0)),
                  pl.BlockSpec((1, 6, d), lambda i: (i // per_seq, 0, 0)),
                  pl.BlockSpec((4, d), lambda i: (0, 0)),
                  pl.BlockSpec((N_EXPERTS, d), lambda i: (0, 0)),
                  pl.BlockSpec((N_EXPERTS, 1), lambda i: (0, 0))],
        out_specs=[pl.BlockSpec((tr, d), lambda i: (i, 0)),
                   pl.BlockSpec((TOP_K, tr), lambda i: (0, i)),
                   pl.BlockSpec((TOP_K, tr), lambda i: (0, i)),
                   pl.BlockSpec((TOP_K, tr), lambda i: (0, i)),
                   pl.BlockSpec((1, N_EXPERTS, LANES), lambda i: (i, 0, 0)),
                   pl.BlockSpec((N_EXPERTS, LANES), lambda i: (0, 0))],
        out_shape=[jax.ShapeDtypeStruct((n, d), BF16),
                   jax.ShapeDtypeStruct((TOP_K, n), I32),
                   jax.ShapeDtypeStruct((TOP_K, n), F32),
                   jax.ShapeDtypeStruct((TOP_K, n), I32),
                   jax.ShapeDtypeStruct((n // tr, N_EXPERTS, LANES), I32),
                   jax.ShapeDtypeStruct((N_EXPERTS, LANES), I32)],
        scratch_shapes=[pltpu.VMEM((N_EXPERTS, LANES), F32)],
        compiler_params=_cparams(("arbitrary",)),
        name="moe_route",
    )(x2d, mod, ng, router_w.T, router_b.reshape(N_EXPERTS, 1))


def _window_copy(src, dst, sem):
    return pltpu.make_async_copy(src, dst, sem)


def _rows_at(ref, row, n_rows):
    return ref.at[pl.ds(pl.multiple_of(row * SUBLANES, SUBLANES), n_rows * SUBLANES), :]


def _to_tiles(dst, val):
    rows = val.shape[0]
    for jj in range(val.shape[1] // LANES):
        dst[pl.ds(jj, rows, stride=SUBLANES), :] = val[:, jj * LANES:(jj + 1) * LANES]


def _from_tiles(src, rows):
    return jnp.concatenate([src[pl.ds(jj, rows, stride=SUBLANES), :] for jj in range(SUBLANES)],
                           axis=1)


def _dispatch_kernel(start_ref, count_ref, comp_ref, base_ref, ctile_ref, h_ref, eid_ref,
                     rank_ref, hs_ref, win_scr, ovf_scr, zero_scr, sems, sync_sem):
    i = pl.program_id(0)
    n_steps = pl.num_programs(0)
    slot = i % 2
    w = DISPATCH_WINDOW
    t = h_ref.shape[0]
    h = h_ref[...]
    e0 = eid_ref[0:1, :]
    e1 = eid_ref[1:2, :]
    r0 = rank_ref[0:1, :]
    r1 = rank_ref[1:2, :]
    row = lax.broadcasted_iota(I32, (w, t), 0)

    def local_rank(e):
        return jnp.where(e0 == e, r0, jnp.where(e1 == e, r1, -1)) - base_ref[i * N_EXPERTS + e]

    def selector(e, first_row):
        return jnp.where(row + first_row == local_rank(e), 1.0, 0.0).astype(BF16)

    slab = jnp.dot(jnp.concatenate([selector(e, 0) for e in range(N_EXPERTS)], axis=0), h,
                   preferred_element_type=F32)
    for e in range(N_EXPERTS):
        _to_tiles(win_scr.at[slot, e], slab[e * w:(e + 1) * w, :])

    def window(step_slot, e, step):
        dst_row = start_ref[e] + base_ref[step * N_EXPERTS + e]
        return _window_copy(win_scr.at[step_slot, e], _rows_at(hs_ref, dst_row, w),
                            sems.at[step_slot])

    @pl.when(i > 0)
    def _():
        for e in range(N_EXPERTS):
            window(1 - slot, e, i - 1).wait()

    for e in range(N_EXPERTS):
        window(slot, e, i).start()

    for e in range(N_EXPERTS):
        run = ctile_ref[i * N_EXPERTS + e]

        @pl.when(run > w)
        def _():
            def extra(k, c):
                first = k * w
                _to_tiles(ovf_scr, jnp.dot(selector(e, first), h, preferred_element_type=F32))
                dst_row = start_ref[e] + base_ref[i * N_EXPERTS + e] + first
                cp = _window_copy(ovf_scr, _rows_at(hs_ref, dst_row, w), sync_sem)
                cp.start()
                cp.wait()
                return c

            lax.fori_loop(1, (run + w - 1) // w, extra, 0)

    @pl.when(i == n_steps - 1)
    def _():
        for e in range(N_EXPERTS):
            window(slot, e, i).wait()
        zero_scr[...] = jnp.zeros_like(zero_scr)
        tm = zero_scr.shape[0] // SUBLANES

        def zero_fill(row_start):
            return _window_copy(zero_scr, _rows_at(hs_ref, row_start, tm), sync_sem)

        for fills in ([zero_fill(start_ref[e] + count_ref[e]) for e in range(N_EXPERTS)],
                      [zero_fill(start_ref[e] + comp_ref[e]) for e in range(N_EXPERTS)]):
            for cp in fills:
                cp.start()
            for cp in fills:
                cp.wait()
        used = start_ref[N_EXPERTS - 1] + comp_ref[N_EXPERTS - 1] + tm
        n_tail = (hs_ref.shape[0] // SUBLANES - used) // tm

        def tail_start(k, c):
            zero_fill(used + k * tm).start()
            return c

        def tail_wait(k, c):
            zero_fill(used + k * tm).wait()
            return c

        lax.fori_loop(0, n_tail, tail_start, 0)
        lax.fori_loop(0, n_tail, tail_wait, 0)


def _dispatch(scalars, h_bf, eid, rank, n_rows):
    n, d = h_bf.shape
    t = DISPATCH_STEPS * MOVE_ROWS
    w = DISPATCH_WINDOW
    assert n % t == 0 and d == SUBLANES * LANES and w % (2 * SUBLANES) == 0 and w <= MOE_ROWS
    grid_spec = pltpu.PrefetchScalarGridSpec(
        num_scalar_prefetch=5,
        grid=(n // t,),
        in_specs=[pl.BlockSpec((t, d), lambda i, *_: (i, 0)),
                  pl.BlockSpec((TOP_K, t), lambda i, *_: (0, i)),
                  pl.BlockSpec((TOP_K, t), lambda i, *_: (0, i))],
        out_specs=pl.BlockSpec(memory_space=pl.ANY),
        scratch_shapes=[pltpu.VMEM((2, N_EXPERTS, w * SUBLANES, LANES), F32),
                        pltpu.VMEM((w * SUBLANES, LANES), F32),
                        pltpu.VMEM((MOE_ROWS * SUBLANES, LANES), F32),
                        pltpu.SemaphoreType.DMA((2,)),
                        pltpu.SemaphoreType.DMA],
    )
    return pl.pallas_call(
        _dispatch_kernel,
        grid_spec=grid_spec,
        out_shape=jax.ShapeDtypeStruct((n_rows * SUBLANES, LANES), F32),
        compiler_params=_cparams(("arbitrary",)),
        name="moe_dispatch",
    )(*scalars, h_bf, eid, rank)


def _expert_kernel(first, last, te_ref, tv_ref, tb_ref, hs_ref, wg_ref, wu_ref, wd_ref, *rest):
    del te_ref, tb_ref
    rest = list(rest)
    partial_ref = None if first else rest.pop(0)
    cast_in = [] if last else [rest.pop(0) for _ in range(3)]
    y_ref, *cast_out = rest
    i = pl.program_id(0)
    tm = hs_ref.shape[0] // SUBLANES

    @pl.when(tv_ref[i] > 0)
    def _():
        _cast_step(cast_in, cast_out)
        h = _from_tiles(hs_ref, tm).astype(BF16)
        part = _swiglu_chunk(h, wg_ref[0], wu_ref[0], wd_ref[0])
        if partial_ref is not None:
            part = part + partial_ref[...]
        if last:
            _to_tiles(y_ref, part)
        else:
            y_ref[...] = part

    @pl.when(tv_ref[i] == 0)
    def _():
        _cast_step(cast_in, cast_out)
        y_ref[...] = jnp.zeros_like(y_ref)


def _experts(tile_expert, tile_valid, tile_block, hs, first_chunk, wg, wu, wd):
    d = wg.shape[1]
    f = wg.shape[2]
    tm = MOE_ROWS
    fc = FFN_COLS
    n_tiles = hs.shape[0] // (tm * SUBLANES)
    n_fc = f // fc
    tiled_in = pl.BlockSpec((tm * SUBLANES, LANES), lambda i, te, tv, tb: (tb[i], 0))
    tiled_out = pl.BlockSpec((tm * SUBLANES, LANES), lambda i, te, tv, tb: (i, 0))
    plain_in = pl.BlockSpec((tm, d), lambda i, te, tv, tb: (tb[i], 0))
    plain_out = pl.BlockSpec((tm, d), lambda i, te, tv, tb: (i, 0))
    weights = tuple(first_chunk)
    y = None
    for c in range(n_fc):
        first, last = c == 0, c == n_fc - 1
        in_specs = [tiled_in,
                    pl.BlockSpec((1, d, fc), lambda i, te, tv, tb: (te[i], 0, 0)),
                    pl.BlockSpec((1, d, fc), lambda i, te, tv, tb: (te[i], 0, 0)),
                    pl.BlockSpec((1, fc, d), lambda i, te, tv, tb: (te[i], 0, 0))]
        operands = [hs, *weights]
        if not first:
            in_specs.append(plain_in)
            operands.append(y)
        out_specs = [tiled_out if last else plain_out]
        out_shape = [jax.ShapeDtypeStruct(hs.shape if last else (n_tiles * tm, d), F32)]
        if not last:
            chunks = _cast_chunks(n_tiles, d, fc)
            cast_in, cast_out = _cast_specs(c + 1, chunks, d, fc, lambda i, *_: i)
            in_specs += cast_in
            operands += [wg, wu, wd]
            out_specs += cast_out
            out_shape += _cast_shapes(d, fc)
        out = pl.pallas_call(
            functools.partial(_expert_kernel, first, last),
            grid_spec=pltpu.PrefetchScalarGridSpec(
                num_scalar_prefetch=3, grid=(n_tiles,), in_specs=in_specs, out_specs=out_specs),
            out_shape=out_shape,
            compiler_params=_cparams(("arbitrary",)),
            name="moe_experts",
        )(tile_expert, tile_valid, tile_block, *operands)
        y, weights = out[0], tuple(out[1:])
    return y


def _split(v):
    hi = v.astype(BF16)
    lo = (v - hi.astype(F32)).astype(BF16)
    return hi, lo


def _dot3(q, y):
    qh, ql = _split(q)
    yh, yl = _split(y)
    return (jnp.dot(qh, yh, preferred_element_type=F32)
            + jnp.dot(qh, yl, preferred_element_type=F32)
            + jnp.dot(ql, yh, preferred_element_type=F32))


def _combine_kernel(start_ref, comp_ref, base_ref, ctile_ref, eid_ref, rank_ref, gate_ref, x_ref,
                    mod_ref, ng_ref, y_ref, o_ref, win_scr, ovf_scr, f_scr, sems, sync_sem):
    i = pl.program_id(0)
    n_steps = pl.num_programs(0)
    slot = i % 2
    w = COMBINE_WINDOW
    t = x_ref.shape[0]
    e0 = eid_ref[:, 0:1]
    e1 = eid_ref[:, 1:2]
    r0 = rank_ref[:, 0:1]
    r1 = rank_ref[:, 1:2]
    g0 = gate_ref[:, 0:1]
    g1 = gate_ref[:, 1:2]
    col = lax.broadcasted_iota(I32, (t, w), 1)

    def first_row(step, e, second):
        want = base_ref[step * N_EXPERTS + e] + (w if second else 0)
        return start_ref[e] + jnp.minimum(want, comp_ref[e] - w)

    def window(step_slot, e, step):
        return _window_copy(_rows_at(y_ref, first_row(step, e, False), w),
                            win_scr.at[step_slot, e], sems.at[step_slot])

    @pl.when(i == 0)
    def _():
        for e in range(N_EXPERTS):
            window(slot, e, i).start()

    @pl.when(i + 1 < n_steps)
    def _():
        for e in range(N_EXPERTS):
            window(1 - slot, e, i + 1).start()

    for e in range(N_EXPERTS):
        window(slot, e, i).wait()

    def weights(e, second):
        mine0 = e0 == e
        mine1 = e1 == e
        rank = jnp.where(mine0, r0, jnp.where(mine1, r1, -1))
        gate = jnp.where(mine0, g0, jnp.where(mine1, g1, 0.0))
        local = rank - base_ref[i * N_EXPERTS + e]
        in_window = (local >= w) if second else jnp.logical_and(local >= 0, local < w)
        pos = rank + start_ref[e] - first_row(i, e, second)
        return jnp.where(jnp.logical_and(in_window, pos == col), gate, 0.0)

    f = jnp.zeros((t, x_ref.shape[1]), F32)
    for e in range(0, N_EXPERTS, 2):
        q = jnp.concatenate([weights(e, False), weights(e + 1, False)], axis=1)
        y = jnp.concatenate([_from_tiles(win_scr.at[slot, e], w),
                             _from_tiles(win_scr.at[slot, e + 1], w)], axis=0)
        f = f + _dot3(q, y)
    f_scr[...] = f

    for e in range(N_EXPERTS):
        @pl.when(ctile_ref[i * N_EXPERTS + e] > w)
        def _():
            cp = _window_copy(_rows_at(y_ref, first_row(i, e, True), w), ovf_scr, sync_sem)
            cp.start()
            cp.wait()
            f_scr[...] += _dot3(weights(e, True), _from_tiles(ovf_scr, w))

    g2 = mod_ref[0, 5:6, :]
    o_ref[...] = x_ref[...] + g2 * _rms(f_scr[...], ng_ref[3:4, :])


def _combine(scalars, eid_c, rank_c, gate_c, x2d, seq, mod, ng, y):
    n, d = x2d.shape
    t = MOVE_ROWS
    w = COMBINE_WINDOW
    assert seq % t == 0 and t <= 2 * w and w <= MOE_ROWS
    per_seq = seq // t
    grid_spec = pltpu.PrefetchScalarGridSpec(
        num_scalar_prefetch=4,
        grid=(n // t,),
        in_specs=[pl.BlockSpec((t, TOP_K), lambda i, *_: (i, 0)),
                  pl.BlockSpec((t, TOP_K), lambda i, *_: (i, 0)),
                  pl.BlockSpec((t, TOP_K), lambda i, *_: (i, 0)),
                  pl.BlockSpec((t, d), lambda i, *_: (i, 0)),
                  pl.BlockSpec((1, 6, d), lambda i, *_: (i // per_seq, 0, 0)),
                  pl.BlockSpec((4, d), lambda i, *_: (0, 0)),
                  pl.BlockSpec(memory_space=pl.ANY)],
        out_specs=pl.BlockSpec((t, d), lambda i, *_: (i, 0)),
        scratch_shapes=[pltpu.VMEM((2, N_EXPERTS, w * SUBLANES, LANES), F32),
                        pltpu.VMEM((w * SUBLANES, LANES), F32),
                        pltpu.VMEM((t, d), F32),
                        pltpu.SemaphoreType.DMA((2,)),
                        pltpu.SemaphoreType.DMA],
    )
    return pl.pallas_call(
        _combine_kernel,
        grid_spec=grid_spec,
        out_shape=jax.ShapeDtypeStruct((n, d), F32),
        compiler_params=_cparams(("arbitrary",)),
        name="moe_combine",
    )(*scalars, eid_c, rank_c, gate_c, x2d, mod, ng, y)


def _moe(x2d, seq, mod, ng, router_w, router_b, first_chunk, wg, wu, wd):
    n, d = x2d.shape
    tm = MOE_ROWS
    h_bf, eid, gate, rank, base, cnt = _route(x2d, seq, mod, ng, router_w, router_b)
    counts = cnt[:, 0]
    comp = jnp.maximum(((counts + tm - 1) // tm) * tm, tm)
    ends = jnp.cumsum(comp + tm)
    starts = ends - (comp + tm)
    n_tiles = (TOP_K * n) // tm + 2 * N_EXPERTS
    tile_start = jnp.arange(n_tiles, dtype=I32) * tm
    tile_expert = jnp.minimum(jnp.sum(tile_start[:, None] >= ends[None, :], axis=1),
                              N_EXPERTS - 1).astype(I32)
    tile_valid = jnp.logical_and(tile_start < (starts + comp)[tile_expert],
                                 tile_start < ends[-1]).astype(I32)
    base = base[:, :, 0]

    def runs(step_base):
        nxt = jnp.concatenate([step_base[1:], counts[None, :]], axis=0)
        return step_base.reshape(-1), (nxt - step_base).reshape(-1)

    d_base, d_run = runs(base[::DISPATCH_STEPS])
    c_base, c_run = runs(base)
    hs = _dispatch((starts, counts, comp, d_base, d_run), h_bf, eid, rank, n_tiles * tm)
    tile_block = lax.cummax(jnp.where(tile_valid > 0, jnp.arange(n_tiles, dtype=I32), 0))
    y = _experts(tile_expert, tile_valid, tile_block, hs, first_chunk, wg, wu, wd)
    return _combine((starts, comp, c_base, c_run), eid.T, rank.T, gate.T, x2d, seq, mod, ng, y)


def kernel(x, c, w_ada, b_ada, norm_gain, w_in, b_in, ln_v_gain, ln_v_bias, w_spatial, b_spatial, conv_w, conv_b, ln_conv_gain, ln_conv_bias, group_gain, w_out, ffn_w_gate, ffn_w_up, ffn_w_down, router_w, router_b, moe_w_gate, moe_w_up, moe_w_down):
    bsz, seq, d = x.shape
    depth = w_ada.shape[0]
    mod_all = _ada(c, w_ada, b_ada).reshape(depth, bsz, 6, d)
    for l in range(depth):
        mod = mod_all[l]
        ng = norm_gain[l]
        x = _mix(x, mod, ng, w_in[l], b_in[l], ln_v_gain[l], ln_v_bias[l], w_spatial[l],
                 b_spatial[l], conv_w[l], conv_b[l], ln_conv_gain[l], ln_conv_bias[l],
                 group_gain[l], w_out[l])
        x2d = x.reshape(bsz * seq, d)
        i = l // 2
        if l % 2 == 0:
            moe = (moe_w_gate[i], moe_w_up[i], moe_w_down[i]) if l + 1 < depth else None
            x2d, first_chunk = _ffn(x2d, seq, mod, ng, ffn_w_gate[i], ffn_w_up[i], ffn_w_down[i],
                                    moe)
        else:
            x2d = _moe(x2d, seq, mod, ng, router_w[i], router_b[i], first_chunk, moe_w_gate[i],
                       moe_w_up[i], moe_w_down[i])
        x = x2d.reshape(bsz, seq, d)
    return x
```

```python
import functools

import jax
import jax.numpy as jnp
from jax import lax
from jax.experimental import pallas as pl
from jax.experimental.pallas import tpu as pltpu

F32 = jnp.float32
BF16 = jnp.bfloat16
I32 = jnp.int32

EPS = 1e-6
CHUNK = 64
GMLP_BLOCK = 128
N_HEADS_A = 8
CONV_WIDTH = 31
N_EXPERTS = 8
TOP_K = 2

LANES = 128
SUBLANES = 8
CONV_HALO = 32
VMEM_LIMIT = 56 * 1024 * 1024

SEQ_TILE = 512
SUB_TILE = 256
FFN_ROWS = 512
FFN_COLS = 1792
MOE_ROWS = 256
MOVE_ROWS = 256
COMBINE_WINDOW = MOVE_ROWS // 2
DISPATCH_STEPS = 2
DISPATCH_WINDOW = 192


def _rms(x, g):
    return x * lax.rsqrt(jnp.mean(x * x, axis=-1, keepdims=True) + EPS) * g


def _layer_norm(x, g, b):
    mu = jnp.mean(x, axis=-1, keepdims=True)
    xc = x - mu
    return xc * lax.rsqrt(jnp.mean(xc * xc, axis=-1, keepdims=True) + EPS) * g + b


_SQRT_2_OVER_PI = 0.7978845608028654


def _sigmoid(x):
    return 0.5 + 0.5 * jnp.tanh(0.5 * x)


def _silu(x):
    half = 0.5 * x
    return half + half * jnp.tanh(half)


def _gelu(x):
    inner = x * (_SQRT_2_OVER_PI + (_SQRT_2_OVER_PI * 0.044715) * (x * x))
    half = 0.5 * x
    return half + half * jnp.tanh(inner)


def _cparams(sem, vmem=VMEM_LIMIT):
    return pltpu.CompilerParams(dimension_semantics=sem, vmem_limit_bytes=vmem)


def _ada_kernel(c_ref, w_ref, b_ref, o_ref):
    c_hi, c_lo = _split(jax.nn.silu(c_ref[...]))
    w = w_ref[0].astype(BF16)
    o_ref[0] = (jnp.dot(c_hi, w, preferred_element_type=F32)
                + jnp.dot(c_lo, w, preferred_element_type=F32) + b_ref[0])


def _ada(c, w_ada, b_ada):
    depth, d, n6 = w_ada.shape
    bsz = c.shape[0]
    nc = 1536
    return pl.pallas_call(
        _ada_kernel,
        grid=(depth, n6 // nc),
        in_specs=[pl.BlockSpec((bsz, d), lambda l, j: (0, 0)),
                  pl.BlockSpec((1, d, nc), lambda l, j: (l, 0, j)),
                  pl.BlockSpec((1, 1, nc), lambda l, j: (l, 0, j))],
        out_specs=pl.BlockSpec((1, bsz, nc), lambda l, j: (l, 0, j)),
        out_shape=jax.ShapeDtypeStruct((depth, bsz, n6), F32),
        compiler_params=_cparams(("arbitrary", "arbitrary")),
        name="ada_mod",
    )(c, w_ada, b_ada.reshape(depth, 1, n6))


def _mix_kernel(x_ref, mod_ref, ng_ref, w_in_ref, b_in_ref, lnv_g_ref, lnv_b_ref, ws_ref,
                bs_ref, cw_ref, cb_ref, lnc_g_ref, lnc_b_ref, gg_ref, w_out_ref,
                o_ref, wsp_scr, xg_scr, sh_scr, yc_scr):
    ts = x_ref.shape[1]
    d_a = lnv_g_ref.shape[1]
    d_b = lnc_g_ref.shape[1]
    b = pl.program_id(0)
    s = pl.program_id(1)

    @pl.when(jnp.logical_and(b == 0, s == 0))
    def _():
        t_chunk = lax.broadcasted_iota(I32, (GMLP_BLOCK, GMLP_BLOCK), 0) // CHUNK
        s_chunk = lax.broadcasted_iota(I32, (GMLP_BLOCK, GMLP_BLOCK), 1) // CHUNK
        allowed = t_chunk >= s_chunk
        for j in range(N_HEADS_A // 2):
            lo = jnp.where(allowed, ws_ref[2 * j], 0.0).astype(BF16)
            hi = jnp.where(allowed, ws_ref[2 * j + 1], 0.0).astype(BF16)
            wsp_scr[j] = jnp.concatenate([lo, hi], axis=1)

    @pl.when(s == 0)
    def _():
        xg_scr[0:CONV_HALO, :] = jnp.zeros((CONV_HALO, d_b), F32)

    sh1 = mod_ref[0, 0:1, :]
    in_scale = ng_ref[0:1, :] * (1.0 + mod_ref[0, 1:2, :])
    out_scale = mod_ref[0, 2:3, :] * ng_ref[1:2, :]
    head_dim = d_a // N_HEADS_A
    lane = lax.broadcasted_iota(I32, (GMLP_BLOCK, LANES), 1)
    first_head = lane < head_dim
    zero = jnp.zeros((GMLP_BLOCK, LANES), BF16)
    first_tap = CONV_HALO - (CONV_WIDTH - 1)
    keep = CONV_HALO - SUBLANES
    rows = 64
    sub = min(SUB_TILE, ts)

    z_all = []
    for q in range(ts // sub):
        x = x_ref[0, q * sub:(q + 1) * sub, :]
        h = _rms(x, in_scale) + sh1
        z_all.append(jnp.dot(h.astype(BF16), w_in_ref[...], preferred_element_type=F32)
                     + b_in_ref[...])

    def out_proj(lo, ycat):
        y = jnp.dot(ycat, w_out_ref[...], preferred_element_type=F32)
        o_ref[0, lo:lo + sub, :] = x_ref[0, lo:lo + sub, :] + _rms(y, out_scale)

    pending = None
    for q in range(ts // sub):
        lo = q * sub
        z = z_all[q]
        ua = z[:, 0:d_a]
        va = z[:, d_a:2 * d_a]
        ab = z[:, 2 * d_a:2 * d_a + d_b]
        gb = z[:, 2 * d_a + d_b:]

        u = _gelu(ua)
        v = _layer_norm(_gelu(va), lnv_g_ref[...], lnv_b_ref[...]).astype(BF16)
        blocks = []
        for n in range(sub // GMLP_BLOCK):
            cols = []
            for j in range(d_a // LANES):
                vc = v[n * GMLP_BLOCK:(n + 1) * GMLP_BLOCK, j * LANES:(j + 1) * LANES]
                rhs = jnp.concatenate([jnp.where(first_head, vc, zero),
                                       jnp.where(first_head, zero, vc)], axis=0)
                cols.append(jnp.dot(wsp_scr[j], rhs, preferred_element_type=F32))
            blocks.append(jnp.concatenate(cols, axis=1) + bs_ref[...])
        ya = u * jnp.concatenate(blocks, axis=0)

        xg_scr[CONV_HALO + lo:CONV_HALO + lo + sub, :] = ab * _sigmoid(gb)
        if pending is not None:
            out_proj(*pending)
        new_lo = lo if q == 0 else lo + keep
        new_hi = lo + sub + keep
        for r in range(1, SUBLANES):
            sh_scr[r, new_lo:new_hi, :] = xg_scr[new_lo + r:new_hi + r, :]
        for rc in range(sub // rows):
            base = lo + rc * rows
            for lc in range(d_b // LANES):
                ls = slice(lc * LANES, (lc + 1) * LANES)
                acc = jnp.broadcast_to(cb_ref[:, ls], (rows, LANES))
                for k in range(CONV_WIDTH):
                    off = first_tap + k
                    r = off % SUBLANES
                    r0 = base + off - r
                    if r == 0:
                        win = xg_scr[r0:r0 + rows, ls]
                    else:
                        win = sh_scr[r, r0:r0 + rows, ls]
                    acc = acc + cw_ref[k:k + 1, ls] * win
                yc_scr[base:base + rows, ls] = acc
        yb = _silu(_layer_norm(yc_scr[lo:lo + sub, :], lnc_g_ref[...], lnc_b_ref[...]))

        ycat = jnp.concatenate([_rms(ya, gg_ref[:, 0:d_a]), _rms(yb, gg_ref[:, d_a:])], axis=1)
        pending = (lo, ycat.astype(BF16))

    out_proj(*pending)
    xg_scr[0:CONV_HALO, :] = xg_scr[ts:ts + CONV_HALO, :]


def _mix(x, mod, ng, w_in, b_in, lnv_g, lnv_b, w_sp, b_sp, cw, cb, lnc_g, lnc_b, gg, w_out):
    bsz, seq, d = x.shape
    d_in = w_in.shape[1]
    d_a = lnv_g.shape[0]
    d_b = lnc_g.shape[0]
    ts = min(SEQ_TILE, seq)
    assert seq % ts == 0 and ts % GMLP_BLOCK == 0 and ts >= CONV_HALO
    bs_full = jnp.repeat(b_sp.T, d_a // N_HEADS_A, axis=1)
    const = lambda *shape: pl.BlockSpec(shape, lambda b, s: (0,) * len(shape))
    assert ts % min(SUB_TILE, ts) == 0
    return pl.pallas_call(
        _mix_kernel,
        grid=(bsz, seq // ts),
        in_specs=[pl.BlockSpec((1, ts, d), lambda b, s: (b, s, 0)),
                  pl.BlockSpec((1, 6, d), lambda b, s: (b, 0, 0)),
                  const(4, d), const(d, d_in), const(1, d_in), const(1, d_a), const(1, d_a),
                  const(N_HEADS_A, GMLP_BLOCK, GMLP_BLOCK), const(GMLP_BLOCK, d_a),
                  const(CONV_WIDTH, d_b), const(1, d_b), const(1, d_b), const(1, d_b),
                  const(1, d_a + d_b), const(d_a + d_b, d)],
        out_specs=pl.BlockSpec((1, ts, d), lambda b, s: (b, s, 0)),
        out_shape=jax.ShapeDtypeStruct((bsz, seq, d), F32),
        scratch_shapes=[pltpu.VMEM((N_HEADS_A // 2, GMLP_BLOCK, 2 * GMLP_BLOCK), BF16),
                        pltpu.VMEM((ts + CONV_HALO, d_b), F32),
                        pltpu.VMEM((SUBLANES, ts + CONV_HALO, d_b), F32),
                        pltpu.VMEM((ts, d_b), F32)],
        compiler_params=_cparams(("arbitrary", "arbitrary")),
        name="token_mix",
    )(x, mod, ng, w_in.astype(BF16), b_in.reshape(1, d_in), lnv_g.reshape(1, d_a),
      lnv_b.reshape(1, d_a), w_sp, bs_full, cw, cb.reshape(1, d_b), lnc_g.reshape(1, d_b),
      lnc_b.reshape(1, d_b), gg.reshape(1, d_a + d_b), w_out.astype(BF16))


def _swiglu_chunk(h, wg, wu, wd):
    g = jnp.dot(h, wg, preferred_element_type=F32)
    u = jnp.dot(h, wu, preferred_element_type=F32)
    a = (_silu(g) * u).astype(BF16)
    return jnp.dot(a, wd, preferred_element_type=F32)


def _cast_chunks(steps, d, fc):
    for chunks in (8, 4, 2, 1):
        packed_rows = 2 * SUBLANES * chunks
        if N_EXPERTS * chunks <= steps and d % packed_rows == 0 and fc % packed_rows == 0:
            return chunks
    raise ValueError("too few grid steps to convert the expert weights")


def _cast_specs(chunk, chunks, d, fc, step_of):
    last = N_EXPERTS * chunks - 1

    def slab(*grid):
        s = jnp.minimum(step_of(*grid), last)
        return s // chunks, s % chunks

    up = (1, d // chunks, fc)
    down = (1, fc // chunks, d)
    ins = [pl.BlockSpec(up, lambda *g: slab(*g) + (chunk,)),
           pl.BlockSpec(up, lambda *g: slab(*g) + (chunk,)),
           pl.BlockSpec(down, lambda *g: (slab(*g)[0], chunk * chunks + slab(*g)[1], 0))]
    outs = [pl.BlockSpec(up, lambda *g: slab(*g) + (0,)),
            pl.BlockSpec(up, lambda *g: slab(*g) + (0,)),
            pl.BlockSpec(down, lambda *g: slab(*g) + (0,))]
    return ins, outs


def _cast_shapes(d, fc):
    return [jax.ShapeDtypeStruct((N_EXPERTS, d, fc), BF16),
            jax.ShapeDtypeStruct((N_EXPERTS, d, fc), BF16),
            jax.ShapeDtypeStruct((N_EXPERTS, fc, d), BF16)]


def _cast_step(srcs, dsts):
    for src, dst in zip(srcs, dsts):
        dst[...] = src[...].astype(BF16)


def _ffn_kernel(has_cast, x_ref, mod_ref, ng_ref, wg_ref, wu_ref, wd_ref, *rest):
    if has_cast:
        cast_in, (o_ref, *cast_out), (h_scr, acc_scr) = rest[0:3], rest[3:7], rest[7:9]
    else:
        o_ref, h_scr, acc_scr = rest
    j = pl.program_id(1)

    @pl.when(j == 0)
    def _():
        sh2 = mod_ref[0, 3:4, :]
        sc2 = mod_ref[0, 4:5, :]
        h = _rms(x_ref[...], ng_ref[2:3, :] * (1.0 + sc2)) + sh2
        h_scr[...] = h.astype(BF16)
        acc_scr[...] = jnp.zeros_like(acc_scr)

    if has_cast:
        _cast_step(cast_in, cast_out)
    acc_scr[...] += _swiglu_chunk(h_scr[...], wg_ref[...], wu_ref[...], wd_ref[...])

    @pl.when(j == pl.num_programs(1) - 1)
    def _():
        g2 = mod_ref[0, 5:6, :]
        o_ref[...] = x_ref[...] + _rms(acc_scr[...], g2 * ng_ref[3:4, :])


def _ffn(x2d, seq, mod, ng, wg, wu, wd, moe_weights=None):
    n, d = x2d.shape
    f = wg.shape[1]
    tm = min(FFN_ROWS, seq)
    fc = FFN_COLS
    assert seq % tm == 0 and f % fc == 0
    per_seq = seq // tm
    n_fc = f // fc
    in_specs = [pl.BlockSpec((tm, d), lambda i, j: (i, 0)),
                pl.BlockSpec((1, 6, d), lambda i, j: (i // per_seq, 0, 0)),
                pl.BlockSpec((4, d), lambda i, j: (0, 0)),
                pl.BlockSpec((d, fc), lambda i, j: (0, j)),
                pl.BlockSpec((d, fc), lambda i, j: (0, j)),
                pl.BlockSpec((fc, d), lambda i, j: (j, 0))]
    out_specs = [pl.BlockSpec((tm, d), lambda i, j: (i, 0))]
    out_shape = [jax.ShapeDtypeStruct((n, d), F32)]
    operands = [x2d, mod, ng, wg.astype(BF16), wu.astype(BF16), wd.astype(BF16)]
    if moe_weights is not None:
        chunks = _cast_chunks((n // tm) * n_fc, d, fc)
        cast_in, cast_out = _cast_specs(0, chunks, d, fc, lambda i, j: i * n_fc + j)
        in_specs += cast_in
        out_specs += cast_out
        out_shape += _cast_shapes(d, fc)
        operands += list(moe_weights)
    out = pl.pallas_call(
        functools.partial(_ffn_kernel, moe_weights is not None),
        grid=(n // tm, n_fc),
        in_specs=in_specs,
        out_specs=out_specs,
        out_shape=out_shape,
        scratch_shapes=[pltpu.VMEM((tm, d), BF16), pltpu.VMEM((tm, d), F32)],
        compiler_params=_cparams(("arbitrary", "arbitrary")),
        name="ffn_dense",
    )(*operands)
    return out[0], tuple(out[1:])


def _route_kernel(x_ref, mod_ref, ng_ref, rw_ref, rb_ref, h_ref, eid_ref, gate_ref, rank_ref,
                  base_ref, cnt_ref, run_scr):
    i = pl.program_id(0)
    tr = x_ref.shape[0]

    @pl.when(i == 0)
    def _():
        run_scr[...] = jnp.zeros_like(run_scr)

    base_ref[0] = run_scr[...].astype(I32)
    sh2 = mod_ref[0, 3:4, :]
    sc2 = mod_ref[0, 4:5, :]
    h = _rms(x_ref[...], ng_ref[2:3, :] * (1.0 + sc2)) + sh2
    h_hi, h_lo = _split(h)
    h_ref[...] = h_hi
    w_hi, w_lo = _split(rw_ref[...])
    contract_last = (((1,), (1,)), ((), ()))
    logits = (lax.dot_general(w_hi, h_hi, contract_last, preferred_element_type=F32)
              + lax.dot_general(w_hi, h_lo, contract_last, preferred_element_type=F32)
              + lax.dot_general(w_lo, h_hi, contract_last, preferred_element_type=F32)
              + rb_ref[...])
    e_iota = lax.broadcasted_iota(I32, logits.shape, 0)
    m1 = jnp.max(logits, axis=0, keepdims=True)
    i1 = jnp.min(jnp.where(logits == m1, e_iota, N_EXPERTS), axis=0, keepdims=True)
    oh1 = e_iota == i1
    rest = jnp.where(oh1, -jnp.inf, logits)
    m2 = jnp.max(rest, axis=0, keepdims=True)
    i2 = jnp.min(jnp.where(rest == m2, e_iota, N_EXPERTS), axis=0, keepdims=True)
    oh2 = e_iota == i2
    e2 = jnp.exp(m2 - m1)
    den = 1.0 + e2
    gate_ref[...] = jnp.concatenate([1.0 / den, e2 / den], axis=0)
    eid_ref[...] = jnp.concatenate([i1, i2], axis=0)

    member = oh1.astype(F32) + oh2.astype(F32)
    before = (lax.broadcasted_iota(I32, (tr, tr), 0) <
              lax.broadcasted_iota(I32, (tr, tr), 1)).astype(BF16)
    prefix = jnp.dot(member.astype(BF16), before, preferred_element_type=F32) + run_scr[:, 0:1]
    r1 = jnp.sum(jnp.where(oh1, prefix, 0.0), axis=0, keepdims=True)
    r2 = jnp.sum(jnp.where(oh2, prefix, 0.0), axis=0, keepdims=True)
    rank_ref[...] = jnp.concatenate([r1, r2], axis=0).astype(I32)
    run_scr[...] += jnp.sum(member, axis=1, keepdims=True)
    cnt_ref[...] = run_scr[...].astype(I32)


def _route(x2d, seq, mod, ng, router_w, router_b):
    n, d = x2d.shape
    tr = MOVE_ROWS
    assert seq % tr == 0
    per_seq = seq // tr
    return pl.pallas_call(
        _route_kernel,
        grid=(n // tr,),
        in_specs=[pl.BlockSpec((tr, d), lambda i: (i, 0)),
                  pl.BlockSpec((1, 6, d), lambda i: (i // per_seq, 0, 0)),
                  pl.BlockSpec((4, d), lambda i: (0, 0)),
                  pl.BlockSpec((N_EXPERTS, d), lambda i: (0, 0)),
                  pl.BlockSpec((N_EXPERTS, 1), lambda i: (0, 0))],
        out_specs=[pl.BlockSpec((tr, d), lambda i: (i, 0)),
                   pl.BlockSpec((TOP_K, tr), lambda i: (0, i)),
                   pl.BlockSpec((TOP_K, tr), lambda i: (0, i)),
                   pl.BlockSpec((TOP_K, tr), lambda i: (0, i)),
                   pl.BlockSpec((1, N_EXPERTS, LANES), lambda i: (i, 0, 0)),
                   pl.BlockSpec((N_EXPERTS, LANES), lambda i: (0, 0))],
        out_shape=[jax.ShapeDtypeStruct((n, d), BF16),
                   jax.ShapeDtypeStruct((TOP_K, n), I32),
                   jax.ShapeDtypeStruct((TOP_K, n), F32),
                   jax.ShapeDtypeStruct((TOP_K, n), I32),
                   jax.ShapeDtypeStruct((n // tr, N_EXPERTS, LANES), I32),
                   jax.ShapeDtypeStruct((N_EXPERTS, LANES), I32)],
        scratch_shapes=[pltpu.VMEM((N_EXPERTS, LANES), F32)],
        compiler_params=_cparams(("arbitrary",)),
        name="moe_route",
    )(x2d, mod, ng, router_w.T, router_b.reshape(N_EXPERTS, 1))


def _window_copy(src, dst, sem):
    return pltpu.make_async_copy(src, dst, sem)


def _rows_at(ref, row, n_rows):
    return ref.at[pl.ds(pl.multiple_of(row * SUBLANES, SUBLANES), n_rows * SUBLANES), :]


def _to_tiles(dst, val):
    rows = val.shape[0]
    for jj in range(val.shape[1] // LANES):
        dst[pl.ds(jj, rows, stride=SUBLANES), :] = val[:, jj * LANES:(jj + 1) * LANES]


def _from_tiles(src, rows):
    return jnp.concatenate([src[pl.ds(jj, rows, stride=SUBLANES), :] for jj in range(SUBLANES)],
                           axis=1)


def _dispatch_kernel(start_ref, count_ref, comp_ref, base_ref, ctile_ref, h_ref, eid_ref,
                     rank_ref, hs_ref, win_scr, ovf_scr, zero_scr, sems, sync_sem):
    i = pl.program_id(0)
    n_steps = pl.num_programs(0)
    slot = i % 2
    w = DISPATCH_WINDOW
    t = h_ref.shape[0]
    h = h_ref[...]
    e0 = eid_ref[0:1, :]
    e1 = eid_ref[1:2, :]
    r0 = rank_ref[0:1, :]
    r1 = rank_ref[1:2, :]
    row = lax.broadcasted_iota(I32, (w, t), 0)

    def local_rank(e):
        return jnp.where(e0 == e, r0, jnp.where(e1 == e, r1, -1)) - base_ref[i * N_EXPERTS + e]

    def selector(e, first_row):
        return jnp.where(row + first_row == local_rank(e), 1.0, 0.0).astype(BF16)

    slab = jnp.dot(jnp.concatenate([selector(e, 0) for e in range(N_EXPERTS)], axis=0), h,
                   preferred_element_type=F32)
    for e in range(N_EXPERTS):
        _to_tiles(win_scr.at[slot, e], slab[e * w:(e + 1) * w, :])

    def window(step_slot, e, step):
        dst_row = start_ref[e] + base_ref[step * N_EXPERTS + e]
        return _window_copy(win_scr.at[step_slot, e], _rows_at(hs_ref, dst_row, w),
                            sems.at[step_slot])

    @pl.when(i > 0)
    def _():
        for e in range(N_EXPERTS):
            window(1 - slot, e, i - 1).wait()

    for e in range(N_EXPERTS):
        window(slot, e, i).start()

    for e in range(N_EXPERTS):
        run = ctile_ref[i * N_EXPERTS + e]

        @pl.when(run > w)
        def _():
            def extra(k, c):
                first = k * w
                _to_tiles(ovf_scr, jnp.dot(selector(e, first), h, preferred_element_type=F32))
                dst_row = start_ref[e] + base_ref[i * N_EXPERTS + e] + first
                cp = _window_copy(ovf_scr, _rows_at(hs_ref, dst_row, w), sync_sem)
                cp.start()
                cp.wait()
                return c

            lax.fori_loop(1, (run + w - 1) // w, extra, 0)

    @pl.when(i == n_steps - 1)
    def _():
        for e in range(N_EXPERTS):
            window(slot, e, i).wait()
        zero_scr[...] = jnp.zeros_like(zero_scr)
        tm = zero_scr.shape[0] // SUBLANES

        def zero_fill(row_start):
            return _window_copy(zero_scr, _rows_at(hs_ref, row_start, tm), sync_sem)

        for fills in ([zero_fill(start_ref[e] + count_ref[e]) for e in range(N_EXPERTS)],
                      [zero_fill(start_ref[e] + comp_ref[e]) for e in range(N_EXPERTS)]):
            for cp in fills:
                cp.start()
            for cp in fills:
                cp.wait()
        used = start_ref[N_EXPERTS - 1] + comp_ref[N_EXPERTS - 1] + tm
        n_tail = (hs_ref.shape[0] // SUBLANES - used) // tm

        def tail_start(k, c):
            zero_fill(used + k * tm).start()
            return c

        def tail_wait(k, c):
            zero_fill(used + k * tm).wait()
            return c

        lax.fori_loop(0, n_tail, tail_start, 0)
        lax.fori_loop(0, n_tail, tail_wait, 0)


def _dispatch(scalars, h_bf, eid, rank, n_rows):
    n, d = h_bf.shape
    t = DISPATCH_STEPS * MOVE_ROWS
    w = DISPATCH_WINDOW
    assert n % t == 0 and d == SUBLANES * LANES and w % (2 * SUBLANES) == 0 and w <= MOE_ROWS
    grid_spec = pltpu.PrefetchScalarGridSpec(
        num_scalar_prefetch=5,
        grid=(n // t,),
        in_specs=[pl.BlockSpec((t, d), lambda i, *_: (i, 0)),
                  pl.BlockSpec((TOP_K, t), lambda i, *_: (0, i)),
                  pl.BlockSpec((TOP_K, t), lambda i, *_: (0, i))],
        out_specs=pl.BlockSpec(memory_space=pl.ANY),
        scratch_shapes=[pltpu.VMEM((2, N_EXPERTS, w * SUBLANES, LANES), F32),
                        pltpu.VMEM((w * SUBLANES, LANES), F32),
                        pltpu.VMEM((MOE_ROWS * SUBLANES, LANES), F32),
                        pltpu.SemaphoreType.DMA((2,)),
                        pltpu.SemaphoreType.DMA],
    )
    return pl.pallas_call(
        _dispatch_kernel,
        grid_spec=grid_spec,
        out_shape=jax.ShapeDtypeStruct((n_rows * SUBLANES, LANES), F32),
        compiler_params=_cparams(("arbitrary",)),
        name="moe_dispatch",
    )(*scalars, h_bf, eid, rank)


def _expert_kernel(first, last, te_ref, tv_ref, tb_ref, hs_ref, wg_ref, wu_ref, wd_ref, *rest):
    del te_ref, tb_ref
    rest = list(rest)
    partial_ref = None if first else rest.pop(0)
    cast_in = [] if last else [rest.pop(0) for _ in range(3)]
    y_ref, *cast_out = rest
    i = pl.program_id(0)
    tm = hs_ref.shape[0] // SUBLANES

    @pl.when(tv_ref[i] > 0)
    def _():
        _cast_step(cast_in, cast_out)
        h = _from_tiles(hs_ref, tm).astype(BF16)
        part = _swiglu_chunk(h, wg_ref[0], wu_ref[0], wd_ref[0])
        if partial_ref is not None:
            part = part + partial_ref[...]
        y_ref[...] = part

    @pl.when(tv_ref[i] == 0)
    def _():
        _cast_step(cast_in, cast_out)
        y_ref[...] = jnp.zeros_like(y_ref)


def _experts(tile_expert, tile_valid, tile_block, hs, first_chunk, wg, wu, wd):
    d = wg.shape[1]
    f = wg.shape[2]
    tm = MOE_ROWS
    fc = FFN_COLS
    n_tiles = hs.shape[0] // (tm * SUBLANES)
    n_fc = f // fc
    tiled_in = pl.BlockSpec((tm * SUBLANES, LANES), lambda i, te, tv, tb: (tb[i], 0))
    plain_in = pl.BlockSpec((tm, d), lambda i, te, tv, tb: (tb[i], 0))
    plain_out = pl.BlockSpec((tm, d), lambda i, te, tv, tb: (i, 0))
    weights = tuple(first_chunk)
    y = None
    for c in range(n_fc):
        first, last = c == 0, c == n_fc - 1
        in_specs = [tiled_in,
                    pl.BlockSpec((1, d, fc), lambda i, te, tv, tb: (te[i], 0, 0)),
                    pl.BlockSpec((1, d, fc), lambda i, te, tv, tb: (te[i], 0, 0)),
                    pl.BlockSpec((1, fc, d), lambda i, te, tv, tb: (te[i], 0, 0))]
        operands = [hs, *weights]
        if not first:
            in_specs.append(plain_in)
            operands.append(y)
        out_specs = [plain_out]
        out_shape = [jax.ShapeDtypeStruct((n_tiles * tm, d), F32)]
        if not last:
            chunks = _cast_chunks(n_tiles, d, fc)
            cast_in, cast_out = _cast_specs(c + 1, chunks, d, fc, lambda i, *_: i)
            in_specs += cast_in
            operands += [wg, wu, wd]
            out_specs += cast_out
            out_shape += _cast_shapes(d, fc)
        out = pl.pallas_call(
            functools.partial(_expert_kernel, first, last),
            grid_spec=pltpu.PrefetchScalarGridSpec(
                num_scalar_prefetch=3, grid=(n_tiles,), in_specs=in_specs, out_specs=out_specs),
            out_shape=out_shape,
            compiler_params=_cparams(("arbitrary",)),
            name="moe_experts",
        )(tile_expert, tile_valid, tile_block, *operands)
        y, weights = out[0], tuple(out[1:])
    return y


def _split(v):
    hi = v.astype(BF16)
    lo = (v - hi.astype(F32)).astype(BF16)
    return hi, lo


def _dot3(q, y):
    qh, ql = _split(q)
    yh, yl = _split(y)
    return (jnp.dot(qh, yh, preferred_element_type=F32)
            + jnp.dot(qh, yl, preferred_element_type=F32)
            + jnp.dot(ql, yh, preferred_element_type=F32))


def _combine_kernel(start_ref, comp_ref, base_ref, ctile_ref, eid_ref, rank_ref, gate_ref, x_ref,
                    mod_ref, ng_ref, y_ref, o_ref, win_scr, ovf_scr, f_scr, sems, sync_sem):
    i = pl.program_id(0)
    n_steps = pl.num_programs(0)
    slot = i % 2
    w = COMBINE_WINDOW
    t = x_ref.shape[0]
    e0 = eid_ref[:, 0:1]
    e1 = eid_ref[:, 1:2]
    r0 = rank_ref[:, 0:1]
    r1 = rank_ref[:, 1:2]
    g0 = gate_ref[:, 0:1]
    g1 = gate_ref[:, 1:2]
    cover = w - SUBLANES
    col = lax.broadcasted_iota(I32, (t, w), 1)

    def first_row(step, e, k):
        want = base_ref[step * N_EXPERTS + e] + k * cover
        row = start_ref[e] + jnp.minimum(want, comp_ref[e] - w)
        return jnp.bitwise_and(row, -SUBLANES)

    def rows_from(row):
        return y_ref.at[pl.ds(pl.multiple_of(row, SUBLANES), w), :]

    def window(step_slot, e, step):
        return _window_copy(rows_from(first_row(step, e, 0)), win_scr.at[step_slot, e],
                            sems.at[step_slot])

    @pl.when(i == 0)
    def _():
        for e in range(N_EXPERTS):
            window(slot, e, i).start()

    @pl.when(i + 1 < n_steps)
    def _():
        for e in range(N_EXPERTS):
            window(1 - slot, e, i + 1).start()

    for e in range(N_EXPERTS):
        window(slot, e, i).wait()

    def weights(e, k):
        mine0 = e0 == e
        mine1 = e1 == e
        rank = jnp.where(mine0, r0, jnp.where(mine1, r1, -1))
        gate = jnp.where(mine0, g0, jnp.where(mine1, g1, 0.0))
        local = rank - base_ref[i * N_EXPERTS + e]
        in_window = jnp.logical_and(local >= k * cover, local < (k + 1) * cover)
        pos = rank + start_ref[e] - first_row(i, e, k)
        return jnp.where(jnp.logical_and(in_window, pos == col), gate, 0.0)

    f = jnp.zeros((t, x_ref.shape[1]), F32)
    for e in range(0, N_EXPERTS, 2):
        q = jnp.concatenate([weights(e, 0), weights(e + 1, 0)], axis=1)
        y = jnp.concatenate([win_scr[slot, e], win_scr[slot, e + 1]], axis=0)
        f = f + _dot3(q, y)
    f_scr[...] = f

    for e in range(N_EXPERTS):
        run = ctile_ref[i * N_EXPERTS + e]

        @pl.when(run > cover)
        def _():
            def extra(k, c):
                cp = _window_copy(rows_from(first_row(i, e, k)), ovf_scr, sync_sem)
                cp.start()
                cp.wait()
                f_scr[...] += _dot3(weights(e, k), ovf_scr[...])
                return c

            lax.fori_loop(1, (run + cover - 1) // cover, extra, 0)

    g2 = mod_ref[0, 5:6, :]
    o_ref[...] = x_ref[...] + _rms(f_scr[...], g2 * ng_ref[3:4, :])


def _combine(scalars, eid_c, rank_c, gate_c, x2d, seq, mod, ng, y):
    n, d = x2d.shape
    t = MOVE_ROWS
    w = COMBINE_WINDOW
    assert seq % t == 0 and SUBLANES < w <= MOE_ROWS and w % SUBLANES == 0
    per_seq = seq // t
    grid_spec = pltpu.PrefetchScalarGridSpec(
        num_scalar_prefetch=4,
        grid=(n // t,),
        in_specs=[pl.BlockSpec((t, TOP_K), lambda i, *_: (i, 0)),
                  pl.BlockSpec((t, TOP_K), lambda i, *_: (i, 0)),
                  pl.BlockSpec((t, TOP_K), lambda i, *_: (i, 0)),
                  pl.BlockSpec((t, d), lambda i, *_: (i, 0)),
                  pl.BlockSpec((1, 6, d), lambda i, *_: (i // per_seq, 0, 0)),
                  pl.BlockSpec((4, d), lambda i, *_: (0, 0)),
                  pl.BlockSpec(memory_space=pl.ANY)],
        out_specs=pl.BlockSpec((t, d), lambda i, *_: (i, 0)),
        scratch_shapes=[pltpu.VMEM((2, N_EXPERTS, w, d), F32),
                        pltpu.VMEM((w, d), F32),
                        pltpu.VMEM((t, d), F32),
                        pltpu.SemaphoreType.DMA((2,)),
                        pltpu.SemaphoreType.DMA],
    )
    return pl.pallas_call(
        _combine_kernel,
        grid_spec=grid_spec,
        out_shape=jax.ShapeDtypeStruct((n, d), F32),
        compiler_params=_cparams(("arbitrary",)),
        name="moe_combine",
    )(*scalars, eid_c, rank_c, gate_c, x2d, mod, ng, y)


def _moe(x2d, seq, mod, ng, router_w, router_b, first_chunk, wg, wu, wd):
    n, d = x2d.shape
    tm = MOE_ROWS
    h_bf, eid, gate, rank, base, cnt = _route(x2d, seq, mod, ng, router_w, router_b)
    counts = cnt[:, 0]
    comp = jnp.maximum(((counts + tm - 1) // tm) * tm, tm)
    ends = jnp.cumsum(comp + tm)
    starts = ends - (comp + tm)
    n_tiles = (TOP_K * n) // tm + 2 * N_EXPERTS
    tile_start = jnp.arange(n_tiles, dtype=I32) * tm
    tile_expert = jnp.minimum(jnp.sum(tile_start[:, None] >= ends[None, :], axis=1),
                              N_EXPERTS - 1).astype(I32)
    tile_valid = jnp.logical_and(tile_start < (starts + comp)[tile_expert],
                                 tile_start < ends[-1]).astype(I32)
    base = base[:, :, 0]

    def runs(step_base):
        nxt = jnp.concatenate([step_base[1:], counts[None, :]], axis=0)
        return step_base.reshape(-1), (nxt - step_base).reshape(-1)

    d_base, d_run = runs(base[::DISPATCH_STEPS])
    c_base, c_run = runs(base)
    hs = _dispatch((starts, counts, comp, d_base, d_run), h_bf, eid, rank, n_tiles * tm)
    tile_block = lax.cummax(jnp.where(tile_valid > 0, jnp.arange(n_tiles, dtype=I32), 0))
    y = _experts(tile_expert, tile_valid, tile_block, hs, first_chunk, wg, wu, wd)
    return _combine((starts, comp, c_base, c_run), eid.T, rank.T, gate.T, x2d, seq, mod, ng, y)


def kernel(x, c, w_ada, b_ada, norm_gain, w_in, b_in, ln_v_gain, ln_v_bias, w_spatial, b_spatial, conv_w, conv_b, ln_conv_gain, ln_conv_bias, group_gain, w_out, ffn_w_gate, ffn_w_up, ffn_w_down, router_w, router_b, moe_w_gate, moe_w_up, moe_w_down):
    bsz, seq, d = x.shape
    depth = w_ada.shape[0]
    mod_all = _ada(c, w_ada, b_ada).reshape(depth, bsz, 6, d)
    for l in range(depth):
        mod = mod_all[l]
        ng = norm_gain[l]
        x = _mix(x, mod, ng, w_in[l], b_in[l], ln_v_gain[l], ln_v_bias[l], w_spatial[l],
                 b_spatial[l], conv_w[l], conv_b[l], ln_conv_gain[l], ln_conv_bias[l],
                 group_gain[l], w_out[l])
        x2d = x.reshape(bsz * seq, d)
        i = l // 2
        if l % 2 == 0:
            moe = (moe_w_gate[i], moe_w_up[i], moe_w_down[i]) if l + 1 < depth else None
            x2d, first_chunk = _ffn(x2d, seq, mod, ng, ffn_w_gate[i], ffn_w_up[i], ffn_w_down[i],
                                    moe)
        else:
            x2d = _moe(x2d, seq, mod, ng, router_w[i], router_b[i], first_chunk, moe_w_gate[i],
                       moe_w_up[i], moe_w_down[i])
        x = x2d.reshape(bsz, seq, d)
    return x
```
